```python
import jax, jax.numpy as jnp
from jax import lax
import numpy as np

D_MODEL = 1024
BATCH = 4
SEQ = 4096
DEPTH = 4

PLE_DIM = 256
GDN_HEADS = 4
GDN_DK = 128
GDN_DV = 128
CONV_WIDTH = 4
MLSTM_HEADS = 4
MLSTM_DQK = 64
MLSTM_DV = 128
CHUNK = 64
D_FF = -(-(8 * D_MODEL) // (3 * 256)) * 256
GDN_QK = GDN_HEADS * GDN_DK
GDN_V = GDN_HEADS * GDN_DV
ML_QK = MLSTM_HEADS * MLSTM_DQK
ML_V = MLSTM_HEADS * MLSTM_DV
IN_SIZES = (GDN_QK, GDN_QK, GDN_V, GDN_V, GDN_HEADS, GDN_HEADS,
            ML_QK, ML_QK, ML_V, ML_V, MLSTM_HEADS, MLSTM_HEADS,
            D_MODEL, D_MODEL)
IN_WIDTH = 2 * GDN_QK + 2 * GDN_V + 2 * GDN_HEADS + 2 * ML_QK + 2 * ML_V + 2 * MLSTM_HEADS + 2 * D_MODEL
CONV_CH = 2 * GDN_QK + GDN_V
NORM_EPS = 1e-6

kernel_name = 'hybrid_gdn_mlstm_ple'


def _rmsnorm(x, gain):
    xf = x.astype(jnp.float32)
    y = xf * lax.rsqrt(jnp.mean(xf * xf, axis=-1, keepdims=True) + NORM_EPS)
    return (y * gain.astype(jnp.float32)).astype(x.dtype)


def _l2norm(x):
    return x * lax.rsqrt(jnp.sum(x * x, axis=-1, keepdims=True) + NORM_EPS)


def _causal_conv(x, w):
    c = x.shape[-1]
    return lax.conv_general_dilated(x, w[:, None, :].astype(x.dtype), window_strides=(1,),
                                    padding=[(CONV_WIDTH - 1, 0)],
                                    dimension_numbers=('NWC', 'WIO', 'NWC'),
                                    feature_group_count=c)


def _chunk_seq(t):
    b, s, h, d = t.shape
    return t.reshape(b, s // CHUNK, CHUNK, h, d).transpose(1, 0, 3, 2, 4)


def _chunk_gate(t):
    b, s, h = t.shape
    return t.reshape(b, s // CHUNK, CHUNK, h).transpose(1, 0, 3, 2)


def _unchunk(t):
    n, b, h, c, d = t.shape
    return t.transpose(1, 0, 3, 2, 4).reshape(b, n * c, h, d)


def gated_delta_rule(q, k, v, g, beta):
    dk = q.shape[-1]
    dv = v.shape[-1]
    q = _l2norm(q) * (dk ** -0.5)
    k = _l2norm(k)
    qc, kc, vc = _chunk_seq(q), _chunk_seq(k), _chunk_seq(v)
    gc = jnp.cumsum(_chunk_gate(g), axis=-1)
    bc = _chunk_gate(beta)
    incl = jnp.tril(jnp.ones((CHUNK, CHUNK), dtype=bool))
    strict = jnp.tril(jnp.ones((CHUNK, CHUNK), dtype=bool), -1)
    decay = jnp.exp(jnp.where(incl, gc[..., :, None] - gc[..., None, :], -jnp.inf))
    k_beta = kc * bc[..., None]
    m = jnp.where(strict, jnp.einsum('nbhcd,nbhsd->nbhcs', k_beta, kc) * decay, 0.0)
    a = m + jnp.eye(CHUNK, dtype=m.dtype)
    rhs = jnp.concatenate([vc * bc[..., None], k_beta * jnp.exp(gc)[..., None]], axis=-1)
    sol = lax.linalg.triangular_solve(a, rhs, left_side=True, lower=True, unit_diagonal=True)
    u, w = sol[..., :dv], sol[..., dv:]
    attn = jnp.einsum('nbhcd,nbhsd->nbhcs', qc, kc) * decay
    g_last = gc[..., -1]

    def step(state, inp):
        q_i, k_i, u_i, w_i, g_i, gl_i, attn_i = inp
        v_new = u_i - jnp.einsum('bhcd,bhde->bhce', w_i, state)
        o = (jnp.einsum('bhcd,bhde->bhce', q_i * jnp.exp(g_i)[..., None], state)
             + jnp.einsum('bhcs,bhse->bhce', attn_i, v_new))
        k_dec = k_i * jnp.exp(gl_i[..., None] - g_i)[..., None]
        state = state * jnp.exp(gl_i)[..., None, None] + jnp.einsum('bhcd,bhce->bhde', k_dec, v_new)
        return state, o

    init = jnp.zeros(qc.shape[1:3] + (dk, dv), jnp.float32)
    _, out = lax.scan(step, init, (qc, kc, u, w, gc, g_last, attn))
    return _unchunk(out)


def mlstm_chunkwise(q, k, v, i_pre, f_pre):
    dqk = q.shape[-1]
    qc = _chunk_seq(q * (dqk ** -0.5))
    kc, vc = _chunk_seq(k), _chunk_seq(v)
    ic = _chunk_gate(i_pre)
    bc = jnp.cumsum(_chunk_gate(jax.nn.log_sigmoid(f_pre)), axis=-1)
    incl = jnp.tril(jnp.ones((CHUNK, CHUNK), dtype=bool))

    def step(carry, inp):
        c_bar, n_bar, m = carry
        q_i, k_i, v_i, i_i, b_i = inp
        d_log = jnp.where(incl, b_i[..., :, None] - b_i[..., None, :] + i_i[..., None, :], -jnp.inf)
        m_inter = b_i + m[..., None]
        m_t = jnp.maximum(m_inter, jnp.max(d_log, axis=-1))
        w_inter = jnp.exp(m_inter - m_t)
        s = jnp.einsum('bhcd,bhsd->bhcs', q_i, k_i) * jnp.exp(d_log - m_t[..., None])
        num = (w_inter[..., None] * jnp.einsum('bhcd,bhde->bhce', q_i, c_bar)
               + jnp.einsum('bhcs,bhse->bhce', s, v_i))
        den = w_inter * jnp.einsum('bhcd,bhd->bhc', q_i, n_bar) + jnp.sum(s, axis=-1)
        h = num / jnp.maximum(jnp.abs(den), jnp.exp(-m_t))[..., None]
        b_last = b_i[..., -1]
        a_i = b_last[..., None] - b_i + i_i
        m_new = jnp.maximum(b_last + m, jnp.max(a_i, axis=-1))
        scale_prev = jnp.exp(b_last + m - m_new)
        k_w = k_i * jnp.exp(a_i - m_new[..., None])[..., None]
        c_bar = scale_prev[..., None, None] * c_bar + jnp.einsum('bhcd,bhce->bhde', k_w, v_i)
        n_bar = scale_prev[..., None] * n_bar + jnp.sum(k_w, axis=-2)
        return (c_bar, n_bar, m_new), h

    nb, bb, hh = qc.shape[0], qc.shape[1], qc.shape[2]
    init = (jnp.zeros((bb, hh, dqk, vc.shape[-1]), jnp.float32),
            jnp.zeros((bb, hh, dqk), jnp.float32),
            jnp.zeros((bb, hh), jnp.float32))
    _, out = lax.scan(step, init, (qc, kc, vc, ic, bc))
    return _unchunk(out)


def _head_rmsnorm(x, gain):
    return x * lax.rsqrt(jnp.mean(x * x, axis=-1, keepdims=True) + NORM_EPS) * gain.astype(jnp.float32)


def _layer(x, p_i, g_mix, w_in, conv_w, a_log, dt_bias, gdn_norm, ml_i_bias, ml_f_bias, ml_norm,
           w_branch_a, w_branch_b, w_out, g_ffn, w1, w3, w2, g_ple, w_ple_gate, w_ple):
    bsz, seq, _ = x.shape
    dt = x.dtype
    h = _rmsnorm(x, g_mix)
    proj = h @ w_in
    idx = [int(v) for v in np.cumsum(IN_SIZES)[:-1]]
    (gq, gk, gv, gz, gbeta, galpha, mq, mk, mv, mo, mi, mf, gate_a, gate_b) = jnp.split(proj, idx, axis=-1)

    qkv = jax.nn.silu(_causal_conv(jnp.concatenate([gq, gk, gv], axis=-1), conv_w))
    gq, gk, gv = jnp.split(qkv.astype(jnp.float32), [GDN_QK, 2 * GDN_QK], axis=-1)
    beta = jax.nn.sigmoid(gbeta.astype(jnp.float32))
    g = -jnp.exp(a_log.astype(jnp.float32)) * jax.nn.softplus(galpha.astype(jnp.float32) + dt_bias.astype(jnp.float32))
    o_a = gated_delta_rule(gq.reshape(bsz, seq, GDN_HEADS, GDN_DK),
                           gk.reshape(bsz, seq, GDN_HEADS, GDN_DK),
                           gv.reshape(bsz, seq, GDN_HEADS, GDN_DV), g, beta)
    o_a = _head_rmsnorm(o_a, gdn_norm) * jax.nn.silu(gz.astype(jnp.float32)).reshape(bsz, seq, GDN_HEADS, GDN_DV)
    y_a = o_a.reshape(bsz, seq, GDN_V).astype(dt) @ w_branch_a

    o_b = mlstm_chunkwise(mq.astype(jnp.float32).reshape(bsz, seq, MLSTM_HEADS, MLSTM_DQK),
                          mk.astype(jnp.float32).reshape(bsz, seq, MLSTM_HEADS, MLSTM_DQK),
                          mv.astype(jnp.float32).reshape(bsz, seq, MLSTM_HEADS, MLSTM_DV),
                          mi.astype(jnp.float32) + ml_i_bias.astype(jnp.float32),
                          mf.astype(jnp.float32) + ml_f_bias.astype(jnp.float32))
    o_b = _head_rmsnorm(o_b, ml_norm) * jax.nn.sigmoid(mo.astype(jnp.float32)).reshape(bsz, seq, MLSTM_HEADS, MLSTM_DV)
    y_b = o_b.reshape(bsz, seq, ML_V).astype(dt) @ w_branch_b

    mixed = jax.nn.sigmoid(gate_a) * y_a + jax.nn.sigmoid(gate_b) * y_b
    x = x + mixed @ w_out

    h = _rmsnorm(x, g_ffn)
    x = x + (jax.nn.silu(h @ w1) * (h @ w3)) @ w2

    gate = jax.nn.sigmoid(_rmsnorm(x, g_ple) @ w_ple_gate)
    x = x + gate * (p_i @ w_ple)
    return x


def setup_inputs(seed: int = 0) -> dict:
    key = jax.random.key(seed)
    ks = jax.random.split(key, 24)
    f32 = jnp.float32
    L, D = DEPTH, D_MODEL

    def nrm(k, shape, scale):
        return jax.random.normal(k, shape, f32) * scale

    def gain(k, shape):
        return 1.0 + 0.05 * jax.random.normal(k, shape, f32)

    dt0 = jnp.exp(jax.random.uniform(ks[5], (L, GDN_HEADS), f32, np.log(1e-3), np.log(1e-1)))
    return {
        'x': jax.random.normal(ks[0], (BATCH, SEQ, D), f32),
        'p': jax.random.normal(ks[1], (DEPTH, BATCH, SEQ, PLE_DIM), f32),
        'g_mix': gain(ks[2], (L, D)),
        'w_in': nrm(ks[3], (L, D, IN_WIDTH), D ** -0.5),
        'conv_w': nrm(ks[4], (L, CONV_WIDTH, CONV_CH), CONV_WIDTH ** -0.5),
        'a_log': jnp.log(jax.random.uniform(ks[6], (L, GDN_HEADS), f32, 1.0, 16.0)),
        'dt_bias': dt0 + jnp.log(-jnp.expm1(-dt0)),
        'gdn_norm': gain(ks[7], (L, GDN_DV)),
        'ml_i_bias': nrm(ks[8], (L, MLSTM_HEADS), 0.1),
        'ml_f_bias': jnp.linspace(3.0, 6.0, MLSTM_HEADS, dtype=f32)[None, :] + nrm(ks[9], (L, MLSTM_HEADS), 0.1),
        'ml_norm': gain(ks[10], (L, MLSTM_HEADS, MLSTM_DV)),
        'w_branch_a': nrm(ks[11], (L, GDN_V, D), GDN_V ** -0.5),
        'w_branch_b': nrm(ks[12], (L, ML_V, D), ML_V ** -0.5),
        'w_out': nrm(ks[13], (L, D, D), D ** -0.5),
        'g_ffn': gain(ks[14], (L, D)),
        'w1': nrm(ks[15], (L, D, D_FF), D ** -0.5),
        'w3': nrm(ks[16], (L, D, D_FF), D ** -0.5),
        'w2': nrm(ks[17], (L, D_FF, D), D_FF ** -0.5),
        'g_ple': gain(ks[18], (L, D)),
        'w_ple_gate': nrm(ks[19], (L, D, D), D ** -0.5),
        'w_ple': nrm(ks[20], (L, PLE_DIM, D), PLE_DIM ** -0.5),
        'g_final': gain(ks[21], (D,)),
    }


def reference(x, p, g_mix, w_in, conv_w, a_log, dt_bias, gdn_norm, ml_i_bias, ml_f_bias, ml_norm,
              w_branch_a, w_branch_b, w_out, g_ffn, w1, w3, w2, g_ple, w_ple_gate, w_ple, g_final):
    for i in range(DEPTH):
        x = _layer(x, p[i], g_mix[i], w_in[i], conv_w[i], a_log[i], dt_bias[i], gdn_norm[i],
                   ml_i_bias[i], ml_f_bias[i], ml_norm[i], w_branch_a[i], w_branch_b[i], w_out[i],
                   g_ffn[i], w1[i], w3[i], w2[i], g_ple[i], w_ple_gate[i], w_ple[i])
    return _rmsnorm(x, g_final)
```

```python
import functools

import jax
import jax.numpy as jnp
import numpy as np
from jax import lax
from jax.experimental import pallas as pl
from jax.experimental.pallas import tpu as pltpu

F32 = jnp.float32
BF16 = jnp.bfloat16

NORM_EPS = 1e-6
CHUNK = 64
HEADS = 4
GDN_DK = 128
GDN_DV = 128
ML_DQK = 64
ML_DV = 128
CONV_WIDTH = 4
HEAD_LANES = 128
STACK = HEADS * CHUNK
SUB = 16
SMALL_W = 128
COL = 512

_GA, _GB, _GQ, _GK, _GV, _GZ, _MQK, _MV, _MO = 0, 2, 4, 5, 6, 7, 8, 9, 10
PACK_W = 11 * COL

VMEM_LIMIT = 56 * 1024 * 1024


def _mm(a, b):
    return jnp.dot(a.astype(BF16), b.astype(BF16), preferred_element_type=F32)


def _mm_nt(a, b):
    return lax.dot_general(a.astype(BF16), b.astype(BF16), (((1,), (1,)), ((), ())),
                           preferred_element_type=F32)


def _mm_tn(a, b):
    return lax.dot_general(a.astype(BF16), b.astype(BF16), (((0,), (0,)), ((), ())),
                           preferred_element_type=F32)


def _rms(x, gain):
    return x * lax.rsqrt(jnp.mean(x * x, axis=-1, keepdims=True) + NORM_EPS) * gain


def _sigmoid(x):
    return 1.0 / (1.0 + jnp.exp(-x))


def _softplus(x):
    return jnp.maximum(x, 0.0) + jnp.log(1.0 + jnp.exp(-jnp.abs(x)))


def _split3(x):
    hi = x.astype(BF16)
    r = x - hi.astype(F32)
    mid = r.astype(BF16)
    lo = (r - mid.astype(F32)).astype(BF16)
    return hi.astype(F32), mid.astype(F32), lo.astype(F32)


def _inproj_kernel(x_ref, g_ref, w_ref, ws_ref, o_ref, os_ref, h_ref):
    @pl.when(pl.program_id(1) == 0)
    def _():
        hb = _rms(x_ref[...], g_ref[...]).astype(BF16)
        h_ref[...] = hb
        os_ref[...] = jnp.dot(hb, ws_ref[...], preferred_element_type=F32)

    o_ref[...] = jnp.dot(h_ref[...], w_ref[...], preferred_element_type=F32).astype(BF16)


def _inproj(x, gain, w_main, w_small, tm, tn):
    t, d = x.shape
    n = w_main.shape[1]
    return pl.pallas_call(
        _inproj_kernel,
        grid=(t // tm, n // tn),
        in_specs=[
            pl.BlockSpec((tm, d), lambda i, j: (i, 0)),
            pl.BlockSpec((1, d), lambda i, j: (0, 0)),
            pl.BlockSpec((d, tn), lambda i, j: (0, j)),
            pl.BlockSpec((d, SMALL_W), lambda i, j: (0, 0)),
        ],
        out_specs=[
            pl.BlockSpec((tm, tn), lambda i, j: (i, j)),
            pl.BlockSpec((tm, SMALL_W), lambda i, j: (i, 0)),
        ],
        out_shape=[
            jax.ShapeDtypeStruct((t, n), BF16),
            jax.ShapeDtypeStruct((t, SMALL_W), F32),
        ],
        scratch_shapes=[pltpu.VMEM((tm, d), BF16)],
        compiler_params=pltpu.CompilerParams(
            dimension_semantics=("parallel", "arbitrary"), vmem_limit_bytes=VMEM_LIMIT),
        name="inproj",
    )(x, gain, w_main, w_small)


def _stack_heads(a):
    return jnp.concatenate([a[:, HEAD_LANES * h:HEAD_LANES * (h + 1)] for h in range(HEADS)], axis=0)


def _unstack_heads(a):
    return jnp.concatenate([a[CHUNK * h:CHUNK * (h + 1), :] for h in range(HEADS)], axis=1)


def _stack_col(sm, k):
    return jnp.concatenate([sm[:, HEADS * k + h:HEADS * k + h + 1] for h in range(HEADS)], axis=0)


def _block_last(col):
    return jnp.concatenate(
        [jnp.broadcast_to(col[CHUNK * (h + 1) - 1:CHUNK * (h + 1), :], (CHUNK, 1)) for h in range(HEADS)],
        axis=0)


def _block_max(col):
    return jnp.concatenate(
        [jnp.broadcast_to(jnp.max(col[CHUNK * h:CHUNK * (h + 1), :], axis=0, keepdims=True), (CHUNK, 1))
         for h in range(HEADS)], axis=0)


def _lane_cols(cols, width):
    rows = cols[0].shape[0]
    lane = lax.broadcasted_iota(jnp.int32, (rows, width), 1)
    out = jnp.zeros((rows, width), F32)
    for idx, c in enumerate(cols):
        out = jnp.where(lane == idx, c, out)
    return out


def _block_cumsum(col, lbd):
    parts = _lane_cols(list(_split3(col)), HEAD_LANES)
    return jnp.sum(jnp.dot(lbd, parts.astype(BF16), preferred_element_type=F32), axis=-1, keepdims=True)


def _pair_diff(row_col, col_col):
    one = jnp.ones_like(row_col)
    a = _lane_cols(list(_split3(row_col)) + [one, one, one], HEAD_LANES)
    b = _lane_cols([one, one, one] + list(_split3(col_col)), HEAD_LANES)
    return _mm_nt(a, b)


def _gdn_chunk(q, k, v, z, beta_pre, alpha_pre, hc, gnorm, negincl, strict, d16, lbd, s_ref):
    qs, ks, vs = _stack_heads(q), _stack_heads(k), _stack_heads(v)
    qn = qs * (lax.rsqrt(jnp.sum(qs * qs, axis=-1, keepdims=True) + NORM_EPS) * (GDN_DK ** -0.5))
    kn = ks * lax.rsqrt(jnp.sum(ks * ks, axis=-1, keepdims=True) + NORM_EPS)
    beta = _sigmoid(beta_pre)
    g = -jnp.exp(hc[:, 0:1]) * _softplus(alpha_pre + hc[:, 1:2])
    gc = _block_cumsum(g, lbd)
    decay = jnp.exp(_pair_diff(gc, -gc) + negincl)
    gram = _mm_nt(jnp.concatenate([qn, kn], axis=0), kn)
    attn = gram[:STACK] * decay
    m = (beta * gram[STACK:]) * decay * strict
    egc = jnp.exp(gc)
    kb = kn * beta
    rhs = jnp.concatenate([vs * beta, kb * egc], axis=1)

    md = m * d16
    n1 = -md
    n2 = _mm(n1, n1)
    n4 = _mm(n2, n2)
    n8 = _mm(n4, n4)
    zz = jnp.concatenate([m - md, rhs], axis=1)
    for p in (n1, n2, n4, n8):
        zz = zz + _mm(p, zz)
    r = zz[:, :STACK]
    y = zz[:, STACK:]
    y = y + _mm(_mm(r, r), y)
    y = y - _mm(r, y)
    u = y[:, :GDN_DV]
    w = y[:, GDN_DV:]

    qg = qn * egc
    qs_parts, ws_parts = [], []
    for h in range(HEADS):
        rows = slice(CHUNK * h, CHUNK * (h + 1))
        lw = jnp.concatenate([qg[rows], w[rows]], axis=0)
        ph = _mm(lw, s_ref[:, GDN_DV * h:GDN_DV * (h + 1)])
        qs_parts.append(ph[:CHUNK])
        ws_parts.append(ph[CHUNK:])
    v_new = u - jnp.concatenate(ws_parts, axis=0)
    o = jnp.concatenate(qs_parts, axis=0) + _mm(attn, v_new)

    gl = _block_last(gc)
    kdec = kn * jnp.exp(gl - gc)
    egl = jnp.exp(gl)
    for h in range(HEADS):
        rows = slice(CHUNK * h, CHUNK * (h + 1))
        cols = slice(GDN_DV * h, GDN_DV * (h + 1))
        s_ref[:, cols] = s_ref[:, cols] * egl[CHUNK * h:CHUNK * h + 1, :] + _mm_tn(kdec[rows], v_new[rows])

    zs = _stack_heads(z)
    o = o * lax.rsqrt(jnp.mean(o * o, axis=-1, keepdims=True) + NORM_EPS) * gnorm * (zs * _sigmoid(zs))
    return _unstack_heads(o)


def _mlstm_chunk(qk, v, og, i_raw, f_raw, hc, mnorm, negincl, lbd, c_ref, m_ref):
    hw = HEADS * ML_DQK
    q, k = qk[:, :hw], qk[:, hw:]
    lane = lax.broadcasted_iota(jnp.int32, (CHUNK, HEAD_LANES), 1)
    half = [(lane < ML_DQK).astype(F32), (lane >= ML_DQK).astype(F32)]

    def stack_pairs(a):
        return jnp.concatenate(
            [a[:, HEAD_LANES * (h // 2):HEAD_LANES * (h // 2 + 1)] * half[h % 2] for h in range(HEADS)], axis=0)

    qs = stack_pairs(q) * (ML_DQK ** -0.5)
    ks = stack_pairs(k)
    vs = _stack_heads(v)
    ones_col = (lax.broadcasted_iota(jnp.int32, (STACK, ML_DV), 1) == 0).astype(F32)
    vext = jnp.concatenate([vs, ones_col], axis=1)

    i_pre = i_raw + hc[:, 2:3]
    f_pre = f_raw + hc[:, 3:4]
    b = _block_cumsum(-_softplus(-f_pre), lbd)
    dl = _pair_diff(b, i_pre - b) + negincl
    m_prev = m_ref[...]
    m_inter = b + m_prev
    m_t = jnp.maximum(m_inter, jnp.max(dl, axis=-1, keepdims=True))
    w_inter = jnp.exp(m_inter - m_t)
    s = _mm_nt(qs, ks) * jnp.exp(dl - m_t)
    acc = _mm(s, vext)
    qw = qs * w_inter
    inter = jnp.concatenate(
        [_mm(qw[2 * CHUNK * p:2 * CHUNK * (p + 1)], c_ref[p]) for p in range(HEADS // 2)], axis=0)
    acc = acc + inter
    num = acc[:, :ML_DV]
    den = acc[:, ML_DV:ML_DV + 1]
    hout = num / jnp.maximum(jnp.abs(den), jnp.exp(-m_t))

    b_last = _block_last(b)
    a = b_last - b + i_pre
    m_new = jnp.maximum(b_last + m_prev, _block_max(a))
    scale_prev = jnp.exp(b_last + m_prev - m_new)
    kw = ks * jnp.exp(a - m_new)
    for p in range(HEADS // 2):
        rows = slice(2 * CHUNK * p, 2 * CHUNK * (p + 1))
        c_ref[p] = c_ref[p] * scale_prev[rows] + _mm_tn(kw[rows], vext[rows])
    m_ref[...] = m_new

    os_ = _stack_heads(og)
    hout = hout * lax.rsqrt(jnp.mean(hout * hout, axis=-1, keepdims=True) + NORM_EPS) * mnorm * _sigmoid(os_)
    return _unstack_heads(hout)


def _mixer_kernel(gq_ref, gk_ref, gv_ref, gz_ref, mqk_ref, mv_ref, mo_ref, sm_ref,
                  convw_ref, hc_ref, gnorm_ref, mnorm_ref, negincl_ref, strict_ref, d16_ref, lbd_ref,
                  oa_ref, ob_ref, tail_ref, s_ref, c_ref, m_ref):
    @pl.when(pl.program_id(0) == 0)
    def _():
        tail_ref[...] = jnp.zeros_like(tail_ref)
        s_ref[...] = jnp.zeros_like(s_ref)
        c_ref[...] = jnp.zeros_like(c_ref)
        m_ref[...] = jnp.zeros_like(m_ref)

    cw = convw_ref[...]
    hc = hc_ref[...]
    gnorm = gnorm_ref[...]
    mnorm = mnorm_ref[...]
    negincl = negincl_ref[...]
    strict = strict_ref[...]
    d16 = d16_ref[...]
    lbd = lbd_ref[...]
    halo = tail_ref.shape[1]
    for bi in range(gq_ref.shape[0]):
        x = jnp.concatenate([gq_ref[bi], gk_ref[bi], gv_ref[bi]], axis=1).astype(F32)
        xe = jnp.concatenate([tail_ref[bi], x], axis=0)
        tail_ref[bi] = x[CHUNK - halo:, :]
        y = cw[CONV_WIDTH - 1:CONV_WIDTH, :] * x
        for tap in range(1, CONV_WIDTH):
            y = y + cw[CONV_WIDTH - 1 - tap:CONV_WIDTH - tap, :] * pltpu.roll(xe, tap, 0)[halo:, :]
        qkv = y * _sigmoid(y)
        sm = sm_ref[bi]
        hq = HEADS * GDN_DK
        oa = _gdn_chunk(qkv[:, :hq], qkv[:, hq:2 * hq], qkv[:, 2 * hq:], gz_ref[bi].astype(F32),
                        _stack_col(sm, 0), _stack_col(sm, 1), hc, gnorm, negincl, strict, d16, lbd,
                        s_ref.at[bi])
        oa_ref[bi] = oa.astype(BF16)
        ob = _mlstm_chunk(mqk_ref[bi].astype(F32), mv_ref[bi].astype(F32), mo_ref[bi].astype(F32),
                          _stack_col(sm, 2), _stack_col(sm, 3), hc, mnorm, negincl, lbd,
                          c_ref.at[bi], m_ref.at[bi])
        ob_ref[bi] = ob.astype(BF16)


def _mixer_constants():
    idx = np.arange(STACK)
    same_head = (idx[:, None] // CHUNK) == (idx[None, :] // CHUNK)
    incl = same_head & (idx[:, None] >= idx[None, :])
    strict = same_head & (idx[:, None] > idx[None, :])
    d16 = (idx[:, None] // SUB) == (idx[None, :] // SUB)
    negincl = np.where(incl, 0.0, -np.inf).astype(np.float32)
    return (jnp.asarray(negincl), jnp.asarray(strict.astype(np.float32)),
            jnp.asarray(d16.astype(np.float32)), jnp.asarray(incl.astype(np.float32), dtype=BF16))


def _mixer(proj, small, conv_w, hc, gnorm, mnorm, consts):
    bsz, seq, _ = proj.shape
    n_chunks = seq // CHUNK
    conv_ch = conv_w.shape[1]

    def pblk(col):
        return pl.BlockSpec((bsz, CHUNK, COL), lambda c, col=col: (0, c, col))

    def const(shape):
        return pl.BlockSpec(shape, lambda c: (0,) * len(shape))

    return pl.pallas_call(
        _mixer_kernel,
        grid=(n_chunks,),
        in_specs=[pblk(_GQ), pblk(_GK), pblk(_GV), pblk(_GZ), pblk(_MQK), pblk(_MV), pblk(_MO),
                  pl.BlockSpec((bsz, CHUNK, SMALL_W), lambda c: (0, c, 0)),
                  const((CONV_WIDTH, conv_ch)), const((STACK, HEAD_LANES)), const((1, GDN_DV)),
                  const((STACK, ML_DV)), const((STACK, STACK)), const((STACK, STACK)),
                  const((STACK, STACK)), const((STACK, STACK))],
        out_specs=[pl.BlockSpec((bsz, CHUNK, HEADS * GDN_DV), lambda c: (0, c, 0)),
                   pl.BlockSpec((bsz, CHUNK, HEADS * ML_DV), lambda c: (0, c, 0))],
        out_shape=[jax.ShapeDtypeStruct((bsz, seq, HEADS * GDN_DV), BF16),
                   jax.ShapeDtypeStruct((bsz, seq, HEADS * ML_DV), BF16)],
        scratch_shapes=[
            pltpu.VMEM((bsz, 8, conv_ch), F32),
            pltpu.VMEM((bsz, GDN_DK, HEADS * GDN_DV), F32),
            pltpu.VMEM((bsz, HEADS // 2, 2 * ML_DQK, 2 * ML_DV), F32),
            pltpu.VMEM((bsz, STACK, 1), F32),
        ],
        compiler_params=pltpu.CompilerParams(
            dimension_semantics=("arbitrary",), vmem_limit_bytes=VMEM_LIMIT),
        name="mixer",
    )(proj, proj, proj, proj, proj, proj, proj, small, conv_w, hc, gnorm, mnorm, *consts)


def _merge_kernel(oa_ref, ob_ref, ga_ref, gb_ref, x_ref, wa_ref, wb_ref, wo_ref, gffn_ref, x1_ref, h2_ref):
    ya = jnp.dot(oa_ref[...], wa_ref[...], preferred_element_type=F32)
    yb = jnp.dot(ob_ref[...], wb_ref[...], preferred_element_type=F32)
    mixed = _sigmoid(ga_ref[...].astype(F32)) * ya + _sigmoid(gb_ref[...].astype(F32)) * yb
    x1 = x_ref[...] + jnp.dot(mixed.astype(BF16), wo_ref[...], preferred_element_type=F32)
    x1_ref[...] = x1
    h2_ref[...] = _rms(x1, gffn_ref[...]).astype(BF16)


def _merge(oa, ob, proj, x, wa, wb, wo, gffn, tm):
    t, d = x.shape
    gate_w = 2 * COL

    def full(a):
        return pl.BlockSpec(a.shape, lambda i: (0, 0))

    return pl.pallas_call(
        _merge_kernel,
        grid=(t // tm,),
        in_specs=[
            pl.BlockSpec((tm, oa.shape[1]), lambda i: (i, 0)),
            pl.BlockSpec((tm, ob.shape[1]), lambda i: (i, 0)),
            pl.BlockSpec((tm, gate_w), lambda i: (i, _GA // 2)),
            pl.BlockSpec((tm, gate_w), lambda i: (i, _GB // 2)),
            pl.BlockSpec((tm, d), lambda i: (i, 0)),
            full(wa), full(wb), full(wo), full(gffn),
        ],
        out_specs=[pl.BlockSpec((tm, d), lambda i: (i, 0)), pl.BlockSpec((tm, d), lambda i: (i, 0))],
        out_shape=[jax.ShapeDtypeStruct((t, d), F32), jax.ShapeDtypeStruct((t, d), BF16)],
        compiler_params=pltpu.CompilerParams(
            dimension_semantics=("parallel",), vmem_limit_bytes=VMEM_LIMIT),
        name="merge",
    )(oa, ob, proj, proj, x, wa, wb, wo, gffn)


def _ffn_kernel(h_ref, x1_ref, p_ref, w1_ref, w3_ref, w2_ref, gple_ref, wpg_ref, wple_ref, gfin_ref,
                out_ref, acc_ref, *, final):
    j = pl.program_id(1)
    h = h_ref[...]
    a = jnp.dot(h, w1_ref[...], preferred_element_type=F32)
    b = jnp.dot(h, w3_ref[...], preferred_element_type=F32)
    part = jnp.dot((a * _sigmoid(a) * b).astype(BF16), w2_ref[...], preferred_element_type=F32)

    @pl.when(j == 0)
    def _():
        acc_ref[...] = part

    @pl.when(j > 0)
    def _():
        acc_ref[...] += part

    @pl.when(j == pl.num_programs(1) - 1)
    def _():
        x2 = x1_ref[...] + acc_ref[...]
        gate = _sigmoid(jnp.dot(_rms(x2, gple_ref[...]).astype(BF16), wpg_ref[...], preferred_element_type=F32))
        x3 = x2 + gate * jnp.dot(p_ref[...].astype(BF16), wple_ref[...], preferred_element_type=F32)
        if final:
            x3 = _rms(x3, gfin_ref[...])
        out_ref[...] = x3


def _ffn(h2, x1, p, w1, w3, w2, gple, wpg, wple, gfin, tm, tf, final):
    t, d = x1.shape
    dff = w1.shape[1]

    def full(a):
        return pl.BlockSpec(a.shape, lambda i, j: (0, 0))

    return pl.pallas_call(
        functools.partial(_ffn_kernel, final=final),
        grid=(t // tm, dff // tf),
        in_specs=[
            pl.BlockSpec((tm, d), lambda i, j: (i, 0)),
            pl.BlockSpec((tm, d), lambda i, j: (i, 0)),
            pl.BlockSpec((tm, p.shape[1]), lambda i, j: (i, 0)),
            pl.BlockSpec((d, tf), lambda i, j: (0, j)),
            pl.BlockSpec((d, tf), lambda i, j: (0, j)),
            pl.BlockSpec((tf, d), lambda i, j: (j, 0)),
            full(gple), full(wpg), full(wple), full(gfin),
        ],
        out_specs=pl.BlockSpec((tm, d), lambda i, j: (i, 0)),
        out_shape=jax.ShapeDtypeStruct((t, d), F32),
        scratch_shapes=[pltpu.VMEM((tm, d), F32)],
        compiler_params=pltpu.CompilerParams(
            dimension_semantics=("parallel", "arbitrary"), vmem_limit_bytes=VMEM_LIMIT),
        name="ffn_final" if final else "ffn",
    )(h2, x1, p, w1, w3, w2, gple, wpg, wple, gfin)


def _pack_w_in(w_in):
    sizes = (HEADS * GDN_DK, HEADS * GDN_DK, HEADS * GDN_DV, HEADS * GDN_DV, HEADS, HEADS,
             HEADS * ML_DQK, HEADS * ML_DQK, HEADS * ML_DV, HEADS * ML_DV, HEADS, HEADS)
    d = w_in.shape[1]
    sizes = sizes + (d, d)
    assert sum(sizes) == w_in.shape[2]
    offs = np.concatenate([[0], np.cumsum(sizes)])
    seg = [w_in[:, :, int(offs[k]):int(offs[k + 1])] for k in range(len(sizes))]
    gq, gk, gv, gz, gbeta, galpha, mq, mk, mv, mo, mi, mf, gate_a, gate_b = seg
    main = jnp.concatenate([gate_a, gate_b, gq, gk, gv, gz, mq, mk, mv, mo], axis=-1).astype(BF16)
    pad = jnp.zeros(w_in.shape[:2] + (SMALL_W - 4 * HEADS,), w_in.dtype)
    small = jnp.concatenate([gbeta, galpha, mi, mf, pad], axis=-1).astype(BF16)
    return main, small


def kernel(x, p, g_mix, w_in, conv_w, a_log, dt_bias, gdn_norm, ml_i_bias, ml_f_bias, ml_norm, w_branch_a,
           w_branch_b, w_out, g_ffn, w1, w3, w2, g_ple, w_ple_gate, w_ple, g_final):
    bsz, seq, d = x.shape
    depth = w_in.shape[0]
    t = bsz * seq
    assert w_in.shape[2] == PACK_W + 4 * HEADS and seq % CHUNK == 0

    w_main, w_small = _pack_w_in(w_in)
    wa, wb, wo = w_branch_a.astype(BF16), w_branch_b.astype(BF16), w_out.astype(BF16)
    w1b, w3b, w2b = w1.astype(BF16), w3.astype(BF16), w2.astype(BF16)
    wpg, wple = w_ple_gate.astype(BF16), w_ple.astype(BF16)
    hc = jnp.stack([a_log, dt_bias, ml_i_bias, ml_f_bias], axis=-1)
    hc = jnp.repeat(hc, CHUNK, axis=1)
    hc = jnp.pad(hc, ((0, 0), (0, 0), (0, HEAD_LANES - hc.shape[-1])))
    mnorm = jnp.repeat(ml_norm, CHUNK, axis=1)
    consts = _mixer_constants()

    tm_in, tn_in, tm_merge, tm_ffn, tf = min(1024, t), COL, min(512, t), min(512, t), w1.shape[2] // 2
    xt = x.reshape(t, d)
    for i in range(depth):
        proj, small = _inproj(xt, g_mix[i][None, :], w_main[i], w_small[i], tm_in, tn_in)
        oa, ob = _mixer(proj.reshape(bsz, seq, PACK_W), small.reshape(bsz, seq, SMALL_W), conv_w[i], hc[i],
                        gdn_norm[i][None, :], mnorm[i], consts)
        x1, h2 = _merge(oa.reshape(t, -1), ob.reshape(t, -1), proj, xt, wa[i], wb[i], wo[i],
                        g_ffn[i][None, :], tm_merge)
        xt = _ffn(h2, x1, p[i].reshape(t, -1), w1b[i], w3b[i], w2b[i], g_ple[i][None, :], wpg[i], wple[i],
                  g_final[None, :], tm_ffn, tf, final=(i == depth - 1))
    return xt.reshape(bsz, seq, d)
```

```python
import functools

import jax
import jax.numpy as jnp
import numpy as np
from jax import lax
from jax.experimental import pallas as pl
from jax.experimental.pallas import tpu as pltpu

F32 = jnp.float32
BF16 = jnp.bfloat16

NORM_EPS = 1e-6
CHUNK = 64
HEADS = 4
GDN_DK = 128
GDN_DV = 128
ML_DQK = 64
ML_DV = 128
CONV_WIDTH = 4
LANES = 128
PACKED = HEADS * CHUNK
SUB = 16
SMALL_W = 128
COL = 512
HALO = 8

_L_BETA, _L_DECAY, _L_I, _L_F = 0, HEADS, 2 * HEADS, 3 * HEADS

_GA, _GB, _GQ, _GK, _GV, _GZ, _MQK, _MV, _MO = 0, 2, 4, 5, 6, 7, 8, 9, 10
PACK_W = 11 * COL

VMEM_LIMIT = 56 * 1024 * 1024


def _mm(a, b):
    return jnp.dot(a.astype(BF16), b.astype(BF16), preferred_element_type=F32)


def _mm_nt(a, b):
    return lax.dot_general(a.astype(BF16), b.astype(BF16), (((1,), (1,)), ((), ())),
                           preferred_element_type=F32)


def _mm_tn(a, b):
    return lax.dot_general(a.astype(BF16), b.astype(BF16), (((0,), (0,)), ((), ())),
                           preferred_element_type=F32)


def _rms(x, gain):
    return x * lax.rsqrt(jnp.mean(x * x, axis=-1, keepdims=True) + NORM_EPS) * gain


def _sigmoid(x):
    return 0.5 * jnp.tanh(0.5 * x) + 0.5


def _silu(x):
    return x * _sigmoid(x)


def _softplus(x):
    return jnp.maximum(x, 0.0) + jnp.log(1.0 + jnp.exp(-jnp.abs(x)))


def _split3_rows(x):
    hi = x.astype(BF16)
    r = x - hi.astype(F32)
    mid = r.astype(BF16)
    lo = (r - mid.astype(F32)).astype(BF16)
    return jnp.concatenate([hi, mid, lo], axis=0)


def _inproj_kernel(x_ref, g_ref, w_ref, ws_ref, o_ref, os_ref, h_ref):
    @pl.when(pl.program_id(1) == 0)
    def _():
        hb = _rms(x_ref[...], g_ref[...]).astype(BF16)
        h_ref[...] = hb
        os_ref[...] = jnp.dot(hb, ws_ref[...], preferred_element_type=F32)

    o_ref[...] = jnp.dot(h_ref[...], w_ref[...], preferred_element_type=F32).astype(BF16)


def _inproj(x, gain, w_main, w_small, tm):
    t, d = x.shape
    n_col = w_main.shape[0]
    return pl.pallas_call(
        _inproj_kernel,
        grid=(t // tm, n_col),
        in_specs=[
            pl.BlockSpec((tm, d), lambda i, j: (i, 0)),
            pl.BlockSpec((1, d), lambda i, j: (0, 0)),
            pl.BlockSpec((None, d, COL), lambda i, j: (j, 0, 0)),
            pl.BlockSpec((d, SMALL_W), lambda i, j: (0, 0)),
        ],
        out_specs=[
            pl.BlockSpec((None, tm, COL), lambda i, j: (j, i, 0)),
            pl.BlockSpec((tm, SMALL_W), lambda i, j: (i, 0)),
        ],
        out_shape=[
            jax.ShapeDtypeStruct((n_col, t, COL), BF16),
            jax.ShapeDtypeStruct((t, SMALL_W), F32),
        ],
        scratch_shapes=[pltpu.VMEM((tm, d), BF16)],
        compiler_params=pltpu.CompilerParams(
            dimension_semantics=("parallel", "arbitrary"), vmem_limit_bytes=VMEM_LIMIT),
        name="inproj",
    )(x, gain, w_main, w_small)


def _lane_iota(shape):
    return lax.broadcasted_iota(jnp.int32, shape, 1)


def _lane_head(shape):
    return lax.shift_right_logical(_lane_iota(shape), int(np.log2(CHUNK)))


def _expand(blk, lane0):
    low = _lane_iota((CHUNK, LANES)) < CHUNK
    cols = [jnp.broadcast_to(blk[:, lane0 + h:lane0 + h + 1], (CHUNK, LANES)) for h in range(HEADS)]
    return jnp.concatenate([jnp.where(low, cols[2 * p], cols[2 * p + 1]) for p in range(HEADS // 2)], axis=1)


def _row_form(colexp, eyep, ones3):
    return jnp.dot(ones3, _split3_rows(colexp * eyep), preferred_element_type=F32)


def _block_diag(b):
    bb = b.astype(BF16)
    head = _lane_head((CHUNK, PACKED))
    return jnp.concatenate([jnp.where(head == h, bb, jnp.zeros_like(bb)) for h in range(HEADS)], axis=0)


def _pprod(a, b):
    return jnp.dot(a.astype(BF16), _block_diag(b), preferred_element_type=F32)


def _pair_rows(a, p):
    blk = a[:, LANES * p:LANES * (p + 1)]
    low = _lane_iota(blk.shape) < CHUNK
    zero = jnp.zeros_like(blk)
    return jnp.concatenate([jnp.where(low, blk, zero), jnp.where(low, zero, blk)], axis=0)


def _head_cols(a, h, width=LANES):
    return a[:, width * h:width * (h + 1)]


def _round_robin(streams):
    streams = list(streams)
    while streams:
        alive = []
        for g in streams:
            try:
                next(g)
                alive.append(g)
            except StopIteration:
                pass
        streams = alive
        if streams:
            yield


def _gdn_prepare(q, k, v, beta_blk, cs, consts, lw_ref, u_ref, attn_ref, kdec_ref, egl_ref):
    neginclp, strictp, d16p, eyep, ones3 = consts
    qh = [_head_cols(q, h) for h in range(HEADS)]
    kh = [_head_cols(k, h) for h in range(HEADS)]
    qss = [jnp.sum(a * a, axis=-1, keepdims=True) for a in qh]
    kss = [jnp.sum(a * a, axis=-1, keepdims=True) for a in kh]
    egc_blk = jnp.exp(cs)
    gl_row = cs[CHUNK - 1:CHUNK, :]
    kdec_blk = jnp.exp(gl_row - cs)
    egl_row = jnp.exp(gl_row)
    gcol = _expand(cs, _L_DECAY)
    grow = _row_form(gcol, eyep, ones3)
    yield

    qn, kn, kb, qg, vb, kbe, kdec = [], [], [], [], [], [], []
    for h in range(HEADS):
        qsc = lax.rsqrt(qss[h] + NORM_EPS) * (GDN_DK ** -0.5)
        ksc = lax.rsqrt(kss[h] + NORM_EPS)
        beta = beta_blk[:, _L_BETA + h:_L_BETA + h + 1]
        egc = egc_blk[:, _L_DECAY + h:_L_DECAY + h + 1]
        qn.append(qh[h] * qsc)
        kn.append(kh[h] * ksc)
        kb.append(kh[h] * (ksc * beta))
        qg.append(qh[h] * (qsc * egc))
        vb.append(_head_cols(v, h) * beta)
        kbe.append(kh[h] * (ksc * beta * egc))
        kdec.append(kh[h] * (ksc * kdec_blk[:, _L_DECAY + h:_L_DECAY + h + 1]))

    zero = jnp.zeros((CHUNK, LANES), BF16)
    kexp = jnp.concatenate(
        [jnp.concatenate([kn[h].astype(BF16) if g == h else zero for g in range(HEADS)], axis=1)
         for h in range(HEADS)], axis=0)
    lhs = jnp.concatenate([jnp.concatenate(qn, axis=1), jnp.concatenate(kb, axis=1)], axis=0)
    gram = _mm_nt(lhs, kexp)
    decay = jnp.exp(gcol - grow + neginclp)
    yield

    attn = gram[:CHUNK] * decay
    m = gram[CHUNK:] * decay * strictp

    md = m * d16p
    lo = m - md
    n1 = -md
    n2 = _pprod(n1, n1)
    yield
    dgi = eyep + n1
    n4 = _pprod(n2, n2)
    t = _pprod(n2, dgi)
    yield
    dgi = dgi + t
    n8 = _pprod(n4, n4)
    t = _pprod(n4, dgi)
    yield
    dgi = dgi + t
    t = _pprod(n8, dgi)
    yield
    dgi = dgi + t
    r = _pprod(dgi, lo)
    yield
    r2 = _pprod(r, r)
    yield
    t = _pprod(r2, dgi)
    yield
    w1 = dgi + t
    t = _pprod(r, w1)
    yield
    tinv = w1 - t

    pairs = [(2 * p, 2 * p + 1) for p in range(HEADS // 2)]
    sol = []
    for p, heads in enumerate(pairs):
        rhs = jnp.concatenate([jnp.concatenate([vb[h], kbe[h]], axis=1) for h in heads], axis=0)
        sol.append(_mm(_pair_rows(tinv, p), rhs))
    yield
    for h in range(HEADS):
        rows = slice(CHUNK * (h % 2), CHUNK * (h % 2 + 1))
        lw_ref[h] = jnp.concatenate([qg[h], sol[h // 2][rows, GDN_DV:]], axis=0).astype(BF16)
        u_ref[CHUNK * h:CHUNK * (h + 1), :] = sol[h // 2][rows, :GDN_DV]
    for p, heads in enumerate(pairs):
        attn_ref[p] = _pair_rows(attn, p).astype(BF16)
        kdec_ref[p] = jnp.concatenate([kdec[h] for h in heads], axis=0).astype(BF16)
    egl_ref[...] = jnp.broadcast_to(egl_row, egl_ref.shape)


def _gdn_scan(z, gnorm, s_ref, lw_ref, u_ref, attn_ref, kdec_ref, egl_ref, o_ref):
    pairs = [(2 * p, 2 * p + 1) for p in range(HEADS // 2)]
    egl_row = egl_ref[0:1, :]
    ph = [jnp.dot(lw_ref[h], s_ref[:, GDN_DV * h:GDN_DV * (h + 1)].astype(BF16), preferred_element_type=F32)
          for h in range(HEADS)]
    yield
    vnew = [u_ref[CHUNK * h:CHUNK * (h + 1), :] - ph[h][CHUNK:] for h in range(HEADS)]
    intra, upd = [], []
    for p, heads in enumerate(pairs):
        v0, v1 = vnew[heads[0]], vnew[heads[1]]
        intra.append(_mm(attn_ref[p], jnp.concatenate([v0, v1], axis=0)))
        zv = jnp.zeros_like(v0)
        vbd = jnp.concatenate([jnp.concatenate([v0, zv], axis=1), jnp.concatenate([zv, v1], axis=1)], axis=0)
        upd.append(_mm_tn(kdec_ref[p], vbd))
    yield
    out = []
    for p, heads in enumerate(pairs):
        egl = jnp.concatenate(
            [jnp.broadcast_to(egl_row[:, _L_DECAY + h:_L_DECAY + h + 1], (GDN_DK, GDN_DV)) for h in heads], axis=1)
        cols = slice(2 * GDN_DV * p, 2 * GDN_DV * (p + 1))
        s_ref[:, cols] = s_ref[:, cols] * egl + upd[p]
        for idx, h in enumerate(heads):
            out.append(ph[h][:CHUNK] + intra[p][CHUNK * idx:CHUNK * (idx + 1)])
    oss = [jnp.mean(o * o, axis=-1, keepdims=True) for o in out]
    yield
    out = [out[h] * lax.rsqrt(oss[h] + NORM_EPS) * gnorm * _silu(_head_cols(z, h)) for h in range(HEADS)]
    o_ref[...] = jnp.concatenate(out, axis=1).astype(BF16)


def _mlstm_prepare(q, k, gates, cs, consts, qk_ref, dl_ref, gate_ref):
    neginclp, _, _, eyep, ones3 = consts
    head = _lane_head((CHUNK, PACKED))
    kb = k.astype(BF16)
    kexp = jnp.concatenate([jnp.where(head == h, kb, jnp.zeros_like(kb)) for h in range(HEADS)], axis=0)
    qk = _mm_nt(q * (ML_DQK ** -0.5), kexp)
    bcol = _expand(cs, _L_F)
    crow = _row_form(_expand(gates, _L_I) - bcol, eyep, ones3)
    yield

    dl = bcol + crow + neginclp
    lane = _lane_iota((CHUNK, LANES))
    low = lane < CHUNK
    rowmax = jnp.zeros((CHUNK, LANES), F32)
    for h in range(HEADS):
        blk = _head_cols(dl, h // 2)
        mx = jnp.max(jnp.where(low if h % 2 == 0 else ~low, blk, -jnp.inf), axis=-1, keepdims=True)
        rowmax = jnp.where(lane == _L_F + h, mx, rowmax)
    qk_ref[...] = qk
    dl_ref[...] = dl
    gate_ref[0] = cs
    gate_ref[1] = cs[CHUNK - 1:CHUNK, :] - cs + pltpu.roll(gates, _L_F - _L_I, 1)
    gate_ref[2] = rowmax


def _mlstm_scan(q, k, v, og, mnorm, c_ref, m_ref, qk_ref, dl_ref, gate_ref, o_ref):
    cs, a, rowmax = gate_ref[0], gate_ref[1], gate_ref[2]
    m_row = m_ref[0:1, :]
    m_inter = cs + m_row
    m_t = jnp.maximum(m_inter, rowmax)
    w_inter = jnp.exp(m_inter - m_t)
    emt = jnp.exp(-m_t)
    s = qk_ref[...] * jnp.exp(dl_ref[...] - _expand(m_t, _L_F))
    qw = q * (ML_DQK ** -0.5) * _expand(w_inter, _L_F)

    b_last = cs[CHUNK - 1:CHUNK, :]
    m_new = jnp.maximum(b_last + m_row, jnp.max(a, axis=0, keepdims=True))
    scale_row = jnp.exp(b_last + m_row - m_new)
    kw = k * _expand(jnp.exp(a - m_new), _L_F)
    m_ref[...] = jnp.broadcast_to(m_new, m_ref.shape)

    ones_col = (_lane_iota((CHUNK, ML_DV)) == 0).astype(F32)
    pairs = [(2 * p, 2 * p + 1) for p in range(HEADS // 2)]
    acc, upd = [], []
    for p, heads in enumerate(pairs):
        vext = jnp.concatenate(
            [jnp.concatenate([_head_cols(v, h), ones_col], axis=1) for h in heads], axis=0)
        lhs = jnp.concatenate([_pair_rows(qw, p), _pair_rows(s, p)], axis=1)
        acc.append(_mm(lhs, jnp.concatenate([c_ref[p], vext], axis=0)))
        upd.append(_mm_tn(_pair_rows(kw, p), vext))
    yield
    out = []
    for p, heads in enumerate(pairs):
        scale = jnp.concatenate(
            [jnp.broadcast_to(scale_row[:, _L_F + h:_L_F + h + 1], (ML_DQK, 2 * ML_DV)) for h in heads], axis=0)
        c_ref[p] = c_ref[p] * scale + upd[p]
    den = []
    for h in range(HEADS):
        rows = slice(CHUNK * (h % 2), CHUNK * (h % 2 + 1))
        d = jnp.maximum(jnp.abs(acc[h // 2][rows, ML_DV:ML_DV + 1]), emt[:, _L_F + h:_L_F + h + 1])
        den.append(jnp.broadcast_to(d, (CHUNK, ML_DV)))
    yield
    out = [acc[h // 2][CHUNK * (h % 2):CHUNK * (h % 2 + 1), :ML_DV] / den[h] for h in range(HEADS)]
    oss = [jnp.mean(o * o, axis=-1, keepdims=True) for o in out]
    yield
    out = [out[h] * lax.rsqrt(oss[h] + NORM_EPS) * _head_cols(mnorm, h) * _sigmoid(_head_cols(og, h))
           for h in range(HEADS)]
    o_ref[...] = jnp.concatenate(out, axis=1).astype(BF16)


def _mixer_kernel(gq_ref, gk_ref, gv_ref, mqka_ref, sm_ref, gz_ref, mqkb_ref, mv_ref, mo_ref,
                  convw_ref, bias_ref, alog_ref, gnorm_ref, mnorm_ref,
                  negincl_ref, strict_ref, d16_ref, eye_ref, l3_ref,
                  oa_ref, ob_ref, xbuf_ref, s_ref, c_ref, m_ref,
                  lw_ref, u_ref, attn_ref, kdec_ref, egl_ref, qk_ref, dl_ref, gate_ref):
    @pl.when(pl.program_id(0) == 0)
    def _():
        for ref in (xbuf_ref, s_ref, c_ref, m_ref, lw_ref, u_ref, attn_ref, kdec_ref, egl_ref, qk_ref, dl_ref):
            ref[...] = jnp.zeros_like(ref)
        for bi in range(gate_ref.shape[0]):
            gate_ref[bi, 0] = jnp.zeros((CHUNK, SMALL_W), F32)
            gate_ref[bi, 1] = jnp.full((CHUNK, SMALL_W), -jnp.inf, F32)
            gate_ref[bi, 2] = jnp.zeros((CHUNK, SMALL_W), F32)

    cw = convw_ref[...]
    gnorm = gnorm_ref[...]
    mnorm = mnorm_ref[...]
    ones3 = jnp.ones((CHUNK, 3 * CHUNK), BF16)
    consts = (negincl_ref[...], strict_ref[...], d16_ref[...], eye_ref[...], ones3)
    lane = _lane_iota((CHUNK, SMALL_W))
    decay_lanes = (lane >= _L_DECAY) & (lane < _L_DECAY + HEADS)
    f_lanes = (lane >= _L_F) & (lane < _L_F + HEADS)
    neg_a = -jnp.exp(alog_ref[...])
    hq = HEADS * GDN_DK
    hm = HEADS * ML_DQK
    nb = gq_ref.shape[0]
    streams = []
    for bi in range(nb):
        streams.append(_gdn_scan(gz_ref[bi].astype(F32), gnorm, s_ref.at[bi], lw_ref.at[bi], u_ref.at[bi],
                                 attn_ref.at[bi], kdec_ref.at[bi], egl_ref.at[bi], oa_ref.at[bi]))
        mqk = mqkb_ref[bi].astype(F32)
        streams.append(_mlstm_scan(mqk[:, :hm], mqk[:, hm:], mv_ref[bi].astype(F32), mo_ref[bi].astype(F32),
                                   mnorm, c_ref.at[bi], m_ref.at[bi], qk_ref.at[bi], dl_ref.at[bi],
                                   gate_ref.at[bi], ob_ref.at[bi]))

    def prepare(bi):
        x = jnp.concatenate([gq_ref[bi], gk_ref[bi], gv_ref[bi]], axis=1).astype(F32)
        xbuf_ref[bi, HALO:, :] = x
        y = cw[CONV_WIDTH - 1:CONV_WIDTH, :] * x
        for tap in range(1, CONV_WIDTH):
            y = y + cw[CONV_WIDTH - 1 - tap:CONV_WIDTH - tap, :] * xbuf_ref[bi, HALO - tap:HALO - tap + CHUNK, :]
        xbuf_ref[bi, :HALO, :] = x[CHUNK - HALO:, :]
        qkv = _silu(y)

        t = sm_ref[bi] + bias_ref[...]
        sp = _softplus(t)
        beta_blk = _sigmoid(t)
        cs_in = jnp.where(f_lanes, t - sp, jnp.where(decay_lanes, neg_a * sp, 0.0))
        cs = jnp.dot(l3_ref[...], _split3_rows(cs_in), preferred_element_type=F32)
        yield
        mqk = mqka_ref[bi].astype(F32)
        yield from _round_robin([
            _gdn_prepare(qkv[:, :hq], qkv[:, hq:2 * hq], qkv[:, 2 * hq:], beta_blk, cs, consts,
                         lw_ref.at[bi], u_ref.at[bi], attn_ref.at[bi], kdec_ref.at[bi], egl_ref.at[bi]),
            _mlstm_prepare(mqk[:, :hm], mqk[:, hm:], t, cs, consts,
                           qk_ref.at[bi], dl_ref.at[bi], gate_ref.at[bi])])

    streams += [prepare(bi) for bi in range(nb)]
    for _ in _round_robin(streams):
        pass


def _mixer_constants():
    i = np.arange(CHUNK)[:, None]
    j = np.tile(np.arange(CHUNK), HEADS)[None, :]
    negincl = np.where(i >= j, 0.0, -np.inf).astype(np.float32)
    strict = (i > j).astype(np.float32)
    d16 = ((i // SUB) == (j // SUB)).astype(np.float32)
    eye = (i == j).astype(np.float32)
    lower = (np.arange(CHUNK)[:, None] >= np.arange(CHUNK)[None, :]).astype(np.float32)
    l3 = np.tile(lower, (1, 3))
    return (jnp.asarray(negincl), jnp.asarray(strict), jnp.asarray(d16), jnp.asarray(eye),
            jnp.asarray(l3, dtype=BF16))


def _mixer(proj, small, conv_w, bias_row, alog_row, gnorm, mnorm, consts):
    _, bsz, seq, _ = proj.shape
    n_chunks = seq // CHUNK
    conv_ch = conv_w.shape[1]

    def prep(c):
        return jnp.minimum(c, n_chunks - 1)

    def scan(c):
        return jnp.maximum(c - 1, 0)

    def pblk(col, chunk_of):
        return pl.BlockSpec((None, bsz, CHUNK, COL), lambda c: (col, 0, chunk_of(c), 0))

    def const(a):
        return pl.BlockSpec(a.shape, lambda c: (0,) * a.ndim)

    pair = HEADS // 2
    fixed = (conv_w, bias_row, alog_row, gnorm, mnorm) + tuple(consts)
    return pl.pallas_call(
        _mixer_kernel,
        grid=(n_chunks + 1,),
        in_specs=[pblk(_GQ, prep), pblk(_GK, prep), pblk(_GV, prep), pblk(_MQK, prep),
                  pl.BlockSpec((bsz, CHUNK, SMALL_W), lambda c: (0, prep(c), 0)),
                  pblk(_GZ, scan), pblk(_MQK, scan), pblk(_MV, scan), pblk(_MO, scan)]
                 + [const(a) for a in fixed],
        out_specs=[pl.BlockSpec((bsz, CHUNK, HEADS * GDN_DV), lambda c: (0, scan(c), 0)),
                   pl.BlockSpec((bsz, CHUNK, HEADS * ML_DV), lambda c: (0, scan(c), 0))],
        out_shape=[jax.ShapeDtypeStruct((bsz, seq, HEADS * GDN_DV), BF16),
                   jax.ShapeDtypeStruct((bsz, seq, HEADS * ML_DV), BF16)],
        scratch_shapes=[
            pltpu.VMEM((bsz, HALO + CHUNK, conv_ch), F32),
            pltpu.VMEM((bsz, GDN_DK, HEADS * GDN_DV), F32),
            pltpu.VMEM((bsz, pair, 2 * ML_DQK, 2 * ML_DV), F32),
            pltpu.VMEM((bsz, 8, SMALL_W), F32),
            pltpu.VMEM((bsz, HEADS, 2 * CHUNK, GDN_DK), BF16),
            pltpu.VMEM((bsz, HEADS * CHUNK, GDN_DV), F32),
            pltpu.VMEM((bsz, pair, 2 * CHUNK, LANES), BF16),
            pltpu.VMEM((bsz, pair, 2 * CHUNK, GDN_DK), BF16),
            pltpu.VMEM((bsz, 8, SMALL_W), F32),
            pltpu.VMEM((bsz, CHUNK, PACKED), F32),
            pltpu.VMEM((bsz, CHUNK, PACKED), F32),
            pltpu.VMEM((bsz, 3, CHUNK, SMALL_W), F32),
        ],
        compiler_params=pltpu.CompilerParams(
            dimension_semantics=("arbitrary",), vmem_limit_bytes=VMEM_LIMIT),
        name="mixer",
    )(proj, proj, proj, proj, small, proj, proj, proj, proj, *fixed)


def _merge_kernel(oa_ref, ob_ref, ga0_ref, ga1_ref, gb0_ref, gb1_ref, x_ref, wa_ref, wb_ref, wo_ref, gffn_ref,
                  x1_ref, h2_ref):
    ya = jnp.dot(oa_ref[...], wa_ref[...], preferred_element_type=F32)
    yb = jnp.dot(ob_ref[...], wb_ref[...], preferred_element_type=F32)
    ga = jnp.concatenate([ga0_ref[...], ga1_ref[...]], axis=1).astype(F32)
    gb = jnp.concatenate([gb0_ref[...], gb1_ref[...]], axis=1).astype(F32)
    mixed = _sigmoid(ga) * ya + _sigmoid(gb) * yb
    x1 = x_ref[...] + jnp.dot(mixed.astype(BF16), wo_ref[...], preferred_element_type=F32)
    x1_ref[...] = x1
    h2_ref[...] = _rms(x1, gffn_ref[...]).astype(BF16)


def _merge(oa, ob, proj, x, wa, wb, wo, gffn, tm):
    t, d = x.shape

    def full(a):
        return pl.BlockSpec(a.shape, lambda i: (0, 0))

    def gate(col):
        return pl.BlockSpec((None, tm, COL), lambda i: (col, i, 0))

    return pl.pallas_call(
        _merge_kernel,
        grid=(t // tm,),
        in_specs=[
            pl.BlockSpec((tm, oa.shape[1]), lambda i: (i, 0)),
            pl.BlockSpec((tm, ob.shape[1]), lambda i: (i, 0)),
            gate(_GA), gate(_GA + 1), gate(_GB), gate(_GB + 1),
            pl.BlockSpec((tm, d), lambda i: (i, 0)),
            full(wa), full(wb), full(wo), full(gffn),
        ],
        out_specs=[pl.BlockSpec((tm, d), lambda i: (i, 0)), pl.BlockSpec((tm, d), lambda i: (i, 0))],
        out_shape=[jax.ShapeDtypeStruct((t, d), F32), jax.ShapeDtypeStruct((t, d), BF16)],
        compiler_params=pltpu.CompilerParams(
            dimension_semantics=("parallel",), vmem_limit_bytes=VMEM_LIMIT),
        name="merge",
    )(oa, ob, proj, proj, proj, proj, x, wa, wb, wo, gffn)


def _ffn_kernel(h_ref, x1_ref, p_ref, w1_ref, w3_ref, w2_ref, gple_ref, wpg_ref, wple_ref, gfin_ref,
                out_ref, acc_ref, *, final):
    j = pl.program_id(1)
    h = h_ref[...]
    a = jnp.dot(h, w1_ref[...], preferred_element_type=F32)
    b = jnp.dot(h, w3_ref[...], preferred_element_type=F32)
    part = jnp.dot((_silu(a) * b).astype(BF16), w2_ref[...], preferred_element_type=F32)

    @pl.when(j == 0)
    def _():
        acc_ref[...] = part

    @pl.when(j > 0)
    def _():
        acc_ref[...] += part

    @pl.when(j == pl.num_programs(1) - 1)
    def _():
        x2 = x1_ref[...] + acc_ref[...]
        gate = _sigmoid(jnp.dot(_rms(x2, gple_ref[...]).astype(BF16), wpg_ref[...], preferred_element_type=F32))
        x3 = x2 + gate * jnp.dot(p_ref[...].astype(BF16), wple_ref[...], preferred_element_type=F32)
        if final:
            x3 = _rms(x3, gfin_ref[...])
        out_ref[...] = x3


def _ffn(h2, x1, p, w1, w3, w2, gple, wpg, wple, gfin, tm, final):
    t, d = x1.shape
    n_tiles, _, tf = w1.shape

    def full(a):
        return pl.BlockSpec(a.shape, lambda i, j: (0, 0))

    return pl.pallas_call(
        functools.partial(_ffn_kernel, final=final),
        grid=(t // tm, n_tiles),
        in_specs=[
            pl.BlockSpec((tm, d), lambda i, j: (i, 0)),
            pl.BlockSpec((tm, d), lambda i, j: (i, 0)),
            pl.BlockSpec((tm, p.shape[1]), lambda i, j: (i, 0)),
            pl.BlockSpec((None, d, tf), lambda i, j: (j, 0, 0)),
            pl.BlockSpec((None, d, tf), lambda i, j: (j, 0, 0)),
            pl.BlockSpec((tf, d), lambda i, j: (j, 0)),
            full(gple), full(wpg), full(wple), full(gfin),
        ],
        out_specs=pl.BlockSpec((tm, d), lambda i, j: (i, 0)),
        out_shape=jax.ShapeDtypeStruct((t, d), F32),
        scratch_shapes=[pltpu.VMEM((tm, d), F32)],
        compiler_params=pltpu.CompilerParams(
            dimension_semantics=("parallel", "arbitrary"), vmem_limit_bytes=VMEM_LIMIT),
        name="ffn_final" if final else "ffn",
    )(h2, x1, p, w1, w3, w2, gple, wpg, wple, gfin)


def _pack_w_in(w_in):
    sizes = (HEADS * GDN_DK, HEADS * GDN_DK, HEADS * GDN_DV, HEADS * GDN_DV, HEADS, HEADS,
             HEADS * ML_DQK, HEADS * ML_DQK, HEADS * ML_DV, HEADS * ML_DV, HEADS, HEADS)
    d = w_in.shape[1]
    sizes = sizes + (d, d)
    assert sum(sizes) == w_in.shape[2]
    offs = np.concatenate([[0], np.cumsum(sizes)])
    seg = [w_in[:, :, int(offs[k]):int(offs[k + 1])] for k in range(len(sizes))]
    gq, gk, gv, gz, gbeta, galpha, mq, mk, mv, mo, mi, mf, gate_a, gate_b = seg
    main = jnp.concatenate([gate_a, gate_b, gq, gk, gv, gz, mq, mk, mv, mo], axis=-1).astype(BF16)
    main = _col_tiles(main, COL)
    pad = jnp.zeros(w_in.shape[:2] + (SMALL_W - 4 * HEADS,), w_in.dtype)
    small = jnp.concatenate([gbeta, galpha, mi, mf, pad], axis=-1).astype(BF16)
    return main, small


def _col_tiles(w, width):
    depth, d, n = w.shape
    return w.reshape(depth, d, n // width, width).transpose(0, 2, 1, 3)


def _gate_rows(values, lane0):
    return jnp.pad(values, ((0, 0), (lane0, SMALL_W - lane0 - values.shape[1])))[:, None, :]


def kernel(x, p, g_mix, w_in, conv_w, a_log, dt_bias, gdn_norm, ml_i_bias, ml_f_bias, ml_norm, w_branch_a,
           w_branch_b, w_out, g_ffn, w1, w3, w2, g_ple, w_ple_gate, w_ple, g_final):
    bsz, seq, d = x.shape
    depth = w_in.shape[0]
    t = bsz * seq
    assert w_in.shape[2] == PACK_W + 4 * HEADS and seq % CHUNK == 0

    w_main, w_small = _pack_w_in(w_in)
    wa, wb, wo = w_branch_a.astype(BF16), w_branch_b.astype(BF16), w_out.astype(BF16)
    tf = w1.shape[2] // 2
    w1b, w3b, w2b = _col_tiles(w1.astype(BF16), tf), _col_tiles(w3.astype(BF16), tf), w2.astype(BF16)
    wpg, wple = w_ple_gate.astype(BF16), w_ple.astype(BF16)
    bias_rows = _gate_rows(dt_bias, _L_DECAY) + _gate_rows(ml_i_bias, _L_I) + _gate_rows(ml_f_bias, _L_F)
    alog_rows = _gate_rows(a_log, _L_DECAY)
    gnorm = gdn_norm[:, None, :]
    mnorm = ml_norm.reshape(depth, 1, HEADS * ML_DV)
    consts = _mixer_constants()

    tm_in, tm_merge, tm_ffn = min(1024, t), min(512, t), min(512, t)
    xt = x.reshape(t, d)
    for i in range(depth):
        proj, small = _inproj(xt, g_mix[i][None, :], w_main[i], w_small[i], tm_in)
        oa, ob = _mixer(proj.reshape(-1, bsz, seq, COL), small.reshape(bsz, seq, SMALL_W), conv_w[i],
                        bias_rows[i], alog_rows[i], gnorm[i], mnorm[i], consts)
        x1, h2 = _merge(oa.reshape(t, -1), ob.reshape(t, -1), proj, xt, wa[i], wb[i], wo[i],
                        g_ffn[i][None, :], tm_merge)
        xt = _ffn(h2, x1, p[i].reshape(t, -1), w1b[i], w3b[i], w2b[i], g_ple[i][None, :], wpg[i], wple[i],
                  g_final[None, :], tm_ffn, final=(i == depth - 1))
    return xt.reshape(bsz, seq, d)
```

```python
import functools

import jax
import jax.numpy as jnp
import numpy as np
from jax import lax
from jax.experimental import pallas as pl
from jax.experimental.pallas import tpu as pltpu

F32 = jnp.float32
BF16 = jnp.bfloat16

NORM_EPS = 1e-6
CHUNK = 64
HEADS = 4
GDN_DK = 128
GDN_DV = 128
ML_DQK = 64
ML_DV = 128
CONV_WIDTH = 4
LANES = 128
PACKED = HEADS * CHUNK
SUB = 16
SMALL_W = 128
COL = 512
HALO = 8

_L_BETA, _L_DECAY, _L_I, _L_F = 0, HEADS, 2 * HEADS, 3 * HEADS

_GA, _GB, _GQ, _GK, _GV, _GZ, _MQK, _MV, _MO = 0, 2, 4, 5, 6, 7, 8, 9, 10
PACK_W = 11 * COL

VMEM_LIMIT = 56 * 1024 * 1024


def _mm(a, b):
    return jnp.dot(a.astype(BF16), b.astype(BF16), preferred_element_type=F32)


def _mm_nt(a, b):
    return lax.dot_general(a.astype(BF16), b.astype(BF16), (((1,), (1,)), ((), ())),
                           preferred_element_type=F32)


def _mm_tn(a, b):
    return lax.dot_general(a.astype(BF16), b.astype(BF16), (((0,), (0,)), ((), ())),
                           preferred_element_type=F32)


def _rms(x, gain):
    return x * lax.rsqrt(jnp.mean(x * x, axis=-1, keepdims=True) + NORM_EPS) * gain


def _sigmoid(x):
    return 0.5 * jnp.tanh(0.5 * x) + 0.5


def _silu(x):
    h = 0.5 * x
    return h * jnp.tanh(h) + h


def _softplus(x):
    return jnp.maximum(x, 0.0) + jnp.log(1.0 + jnp.exp(-jnp.abs(x)))


def _split3_rows(x):
    hi = x.astype(BF16)
    r = x - hi.astype(F32)
    mid = r.astype(BF16)
    lo = (r - mid.astype(F32)).astype(BF16)
    return jnp.concatenate([hi, mid, lo], axis=0)


def _inproj_kernel(x_ref, g_ref, w_ref, ws_ref, o_ref, os_ref):
    hb = _rms(x_ref[...], g_ref[...]).astype(BF16)
    os_ref[...] = jnp.dot(hb, ws_ref[...], preferred_element_type=F32)
    for k in range(o_ref.shape[0]):
        o_ref[k] = jnp.dot(hb, w_ref[:, COL * k:COL * (k + 1)], preferred_element_type=F32).astype(BF16)


def _inproj(x, gain, w_main, w_small, tm):
    t, d = x.shape
    n_col = w_main.shape[1] // COL
    return pl.pallas_call(
        _inproj_kernel,
        grid=(t // tm,),
        in_specs=[
            pl.BlockSpec((tm, d), lambda i: (i, 0)),
            pl.BlockSpec((1, d), lambda i: (0, 0)),
            pl.BlockSpec(w_main.shape, lambda i: (0, 0)),
            pl.BlockSpec((d, SMALL_W), lambda i: (0, 0)),
        ],
        out_specs=[
            pl.BlockSpec((n_col, tm, COL), lambda i: (0, i, 0)),
            pl.BlockSpec((tm, SMALL_W), lambda i: (i, 0)),
        ],
        out_shape=[
            jax.ShapeDtypeStruct((n_col, t, COL), BF16),
            jax.ShapeDtypeStruct((t, SMALL_W), F32),
        ],
        compiler_params=pltpu.CompilerParams(
            dimension_semantics=("parallel",), vmem_limit_bytes=VMEM_LIMIT),
        name="inproj",
    )(x, gain, w_main, w_small)


def _lane_iota(shape):
    return lax.broadcasted_iota(jnp.int32, shape, 1)


def _lane_head(shape):
    return lax.shift_right_logical(_lane_iota(shape), int(np.log2(CHUNK)))


def _expand(blk, lane0):
    low = _lane_iota((CHUNK, LANES)) < CHUNK
    cols = [jnp.broadcast_to(blk[:, lane0 + h:lane0 + h + 1], (CHUNK, LANES)) for h in range(HEADS)]
    return jnp.concatenate([jnp.where(low, cols[2 * p], cols[2 * p + 1]) for p in range(HEADS // 2)], axis=1)


def _row_form(colexp, eyep, ones3):
    return jnp.dot(ones3, _split3_rows(colexp * eyep), preferred_element_type=F32)


def _block_diag(b):
    bb = b.astype(BF16)
    head = _lane_head((CHUNK, PACKED))
    return jnp.concatenate([jnp.where(head == h, bb, jnp.zeros_like(bb)) for h in range(HEADS)], axis=0)


def _pprod(a, b):
    return jnp.dot(a.astype(BF16), _block_diag(b), preferred_element_type=F32)


def _pair_rows(a, p):
    blk = a[:, LANES * p:LANES * (p + 1)]
    low = _lane_iota(blk.shape) < CHUNK
    zero = jnp.zeros_like(blk)
    return jnp.concatenate([jnp.where(low, blk, zero), jnp.where(low, zero, blk)], axis=0)


def _head_cols(a, h, width=LANES):
    return a[:, width * h:width * (h + 1)]


def _round_robin(streams):
    streams = list(streams)
    while streams:
        alive = []
        for g in streams:
            try:
                next(g)
                alive.append(g)
            except StopIteration:
                pass
        streams = alive
        if streams:
            yield


def _gdn_prepare(q, k, v, beta_blk, cs, consts, lw_ref, u_ref, attn_ref, kdec_ref, egl_ref):
    neginclp, strictp, d16p, eyep, ones3, headsum = consts
    qh = [_head_cols(q, h) for h in range(HEADS)]
    kh = [_head_cols(k, h) for h in range(HEADS)]
    ssq = jnp.dot(jnp.concatenate([q * q, k * k], axis=1).astype(BF16), headsum, preferred_element_type=F32)
    egc_blk = jnp.exp(cs)
    gl_row = cs[CHUNK - 1:CHUNK, :]
    kdec_blk = jnp.exp(gl_row - cs)
    egl_row = jnp.exp(gl_row)
    gcol = _expand(cs, _L_DECAY)
    grow = _row_form(gcol, eyep, ones3)
    yield

    inv_norm = lax.rsqrt(ssq + NORM_EPS)
    qn, kn, kb, qg, vb, kbe, kdec = [], [], [], [], [], [], []
    for h in range(HEADS):
        qsc = inv_norm[:, h:h + 1] * (GDN_DK ** -0.5)
        ksc = inv_norm[:, HEADS + h:HEADS + h + 1]
        beta = beta_blk[:, _L_BETA + h:_L_BETA + h + 1]
        egc = egc_blk[:, _L_DECAY + h:_L_DECAY + h + 1]
        qn.append(qh[h] * qsc)
        kn.append(kh[h] * ksc)
        kb.append(kh[h] * (ksc * beta))
        qg.append(qh[h] * (qsc * egc))
        vb.append(_head_cols(v, h) * beta)
        kbe.append(kh[h] * (ksc * beta * egc))
        kdec.append(kh[h] * (ksc * kdec_blk[:, _L_DECAY + h:_L_DECAY + h + 1]))

    zero = jnp.zeros((CHUNK, LANES), BF16)
    kexp = jnp.concatenate(
        [jnp.concatenate([kn[h].astype(BF16) if g == h else zero for g in range(HEADS)], axis=1)
         for h in range(HEADS)], axis=0)
    lhs = jnp.concatenate([jnp.concatenate(qn, axis=1), jnp.concatenate(kb, axis=1)], axis=0)
    gram = _mm_nt(lhs, kexp)
    decay = jnp.exp(gcol - grow + neginclp)
    yield

    attn = gram[:CHUNK] * decay
    m = gram[CHUNK:] * decay * strictp

    md = m * d16p
    lo = m - md
    n1 = -md
    n2 = _pprod(n1, n1)
    yield
    dgi = eyep + n1
    n4 = _pprod(n2, n2)
    t = _pprod(n2, dgi)
    yield
    dgi = dgi + t
    n8 = _pprod(n4, n4)
    t = _pprod(n4, dgi)
    yield
    dgi = dgi + t
    t = _pprod(n8, dgi)
    yield
    dgi = dgi + t
    r = _pprod(dgi, lo)
    yield
    r2 = _pprod(r, r)
    yield
    t = _pprod(r2, dgi)
    yield
    w1 = dgi + t
    t = _pprod(r, w1)
    yield
    tinv = w1 - t

    pairs = [(2 * p, 2 * p + 1) for p in range(HEADS // 2)]
    sol = []
    for p, heads in enumerate(pairs):
        rhs = jnp.concatenate([jnp.concatenate([vb[h], kbe[h]], axis=1) for h in heads], axis=0)
        sol.append(_mm(_pair_rows(tinv, p), rhs))
    yield
    for h in range(HEADS):
        rows = slice(CHUNK * (h % 2), CHUNK * (h % 2 + 1))
        lw_ref[h] = jnp.concatenate([qg[h], sol[h // 2][rows, GDN_DV:]], axis=0).astype(BF16)
        u_ref[CHUNK * h:CHUNK * (h + 1), :] = sol[h // 2][rows, :GDN_DV]
    for p, heads in enumerate(pairs):
        attn_ref[p] = _pair_rows(attn, p).astype(BF16)
        kdec_ref[p] = jnp.concatenate([kdec[h] for h in heads], axis=0).astype(BF16)
    egl_ref[...] = jnp.broadcast_to(egl_row, egl_ref.shape)


def _gdn_scan(z, gnorm, s_ref, lw_ref, u_ref, attn_ref, kdec_ref, egl_ref, o_ref):
    pairs = [(2 * p, 2 * p + 1) for p in range(HEADS // 2)]
    egl_row = egl_ref[0:1, :]
    ph = [jnp.dot(lw_ref[h], s_ref[:, GDN_DV * h:GDN_DV * (h + 1)].astype(BF16), preferred_element_type=F32)
          for h in range(HEADS)]
    yield
    vnew = [u_ref[CHUNK * h:CHUNK * (h + 1), :] - ph[h][CHUNK:] for h in range(HEADS)]
    intra, upd = [], []
    for p, heads in enumerate(pairs):
        v0, v1 = vnew[heads[0]], vnew[heads[1]]
        intra.append(_mm(attn_ref[p], jnp.concatenate([v0, v1], axis=0)))
        zv = jnp.zeros_like(v0)
        vbd = jnp.concatenate([jnp.concatenate([v0, zv], axis=1), jnp.concatenate([zv, v1], axis=1)], axis=0)
        upd.append(_mm_tn(kdec_ref[p], vbd))
    yield
    out = []
    for p, heads in enumerate(pairs):
        egl = jnp.concatenate(
            [jnp.broadcast_to(egl_row[:, _L_DECAY + h:_L_DECAY + h + 1], (GDN_DK, GDN_DV)) for h in heads], axis=1)
        cols = slice(2 * GDN_DV * p, 2 * GDN_DV * (p + 1))
        s_ref[:, cols] = s_ref[:, cols] * egl + upd[p]
        for idx, h in enumerate(heads):
            out.append(ph[h][:CHUNK] + intra[p][CHUNK * idx:CHUNK * (idx + 1)])
    oss = [jnp.mean(o * o, axis=-1, keepdims=True) for o in out]
    yield
    out = [out[h] * lax.rsqrt(oss[h] + NORM_EPS) * gnorm * _silu(_head_cols(z, h)) for h in range(HEADS)]
    o_ref[...] = jnp.concatenate(out, axis=1).astype(BF16)


def _mlstm_prepare(q, k, gates, cs, consts, qk_ref, dl_ref, gate_ref):
    neginclp, _, _, eyep, ones3, _ = consts
    head = _lane_head((CHUNK, PACKED))
    kb = k.astype(BF16)
    kexp = jnp.concatenate([jnp.where(head == h, kb, jnp.zeros_like(kb)) for h in range(HEADS)], axis=0)
    qk = _mm_nt(q * (ML_DQK ** -0.5), kexp)
    bcol = _expand(cs, _L_F)
    crow = _row_form(_expand(gates, _L_I) - bcol, eyep, ones3)
    yield

    dl = bcol + crow + neginclp
    lane = _lane_iota((CHUNK, LANES))
    low = lane < CHUNK
    rowmax = jnp.zeros((CHUNK, LANES), F32)
    for h in range(HEADS):
        blk = _head_cols(dl, h // 2)
        mx = jnp.max(jnp.where(low if h % 2 == 0 else ~low, blk, -jnp.inf), axis=-1, keepdims=True)
        rowmax = jnp.where(lane == _L_F + h, mx, rowmax)
    qk_ref[...] = qk
    dl_ref[...] = dl
    gate_ref[0] = cs
    gate_ref[1] = cs[CHUNK - 1:CHUNK, :] - cs + pltpu.roll(gates, _L_F - _L_I, 1)
    gate_ref[2] = rowmax


def _mlstm_scan(q, k, v, og, mnorm, c_ref, m_ref, qk_ref, dl_ref, gate_ref, o_ref):
    cs, a, rowmax = gate_ref[0], gate_ref[1], gate_ref[2]
    m_row = m_ref[0:1, :]
    m_inter = cs + m_row
    m_t = jnp.maximum(m_inter, rowmax)
    w_inter = jnp.exp(m_inter - m_t)
    emt = jnp.exp(-m_t)
    s = qk_ref[...] * jnp.exp(dl_ref[...] - _expand(m_t, _L_F))
    qw = q * (ML_DQK ** -0.5) * _expand(w_inter, _L_F)

    b_last = cs[CHUNK - 1:CHUNK, :]
    m_new = jnp.maximum(b_last + m_row, jnp.max(a, axis=0, keepdims=True))
    scale_row = jnp.exp(b_last + m_row - m_new)
    kw = k * _expand(jnp.exp(a - m_new), _L_F)
    m_ref[...] = jnp.broadcast_to(m_new, m_ref.shape)

    ones_col = (_lane_iota((CHUNK, ML_DV)) == 0).astype(F32)
    pairs = [(2 * p, 2 * p + 1) for p in range(HEADS // 2)]
    acc, upd = [], []
    for p, heads in enumerate(pairs):
        vext = jnp.concatenate(
            [jnp.concatenate([_head_cols(v, h), ones_col], axis=1) for h in heads], axis=0)
        lhs = jnp.concatenate([_pair_rows(qw, p), _pair_rows(s, p)], axis=1)
        acc.append(_mm(lhs, jnp.concatenate([c_ref[p], vext], axis=0)))
        upd.append(_mm_tn(_pair_rows(kw, p), vext))
    yield
    out = []
    for p, heads in enumerate(pairs):
        scale = jnp.concatenate(
            [jnp.broadcast_to(scale_row[:, _L_F + h:_L_F + h + 1], (ML_DQK, 2 * ML_DV)) for h in heads], axis=0)
        c_ref[p] = c_ref[p] * scale + upd[p]
    den = []
    for h in range(HEADS):
        rows = slice(CHUNK * (h % 2), CHUNK * (h % 2 + 1))
        d = jnp.maximum(jnp.abs(acc[h // 2][rows, ML_DV:ML_DV + 1]), emt[:, _L_F + h:_L_F + h + 1])
        den.append(jnp.broadcast_to(d, (CHUNK, ML_DV)))
    yield
    out = [acc[h // 2][CHUNK * (h % 2):CHUNK * (h % 2 + 1), :ML_DV] / den[h] for h in range(HEADS)]
    oss = [jnp.mean(o * o, axis=-1, keepdims=True) for o in out]
    yield
    out = [out[h] * lax.rsqrt(oss[h] + NORM_EPS) * _head_cols(mnorm, h) * _sigmoid(_head_cols(og, h))
           for h in range(HEADS)]
    o_ref[...] = jnp.concatenate(out, axis=1).astype(BF16)


def _mixer_kernel(gq_ref, gk_ref, gv_ref, mqka_ref, sm_ref, gz_ref, mqkb_ref, mv_ref, mo_ref,
                  convw_ref, bias_ref, alog_ref, gnorm_ref, mnorm_ref,
                  negincl_ref, strict_ref, d16_ref, eye_ref, l3_ref, headsum_ref,
                  oa_ref, ob_ref, xbuf_ref, s_ref, c_ref, m_ref,
                  lw_ref, u_ref, attn_ref, kdec_ref, egl_ref, qk_ref, dl_ref, gate_ref):
    @pl.when(pl.program_id(0) == 0)
    def _():
        for ref in (xbuf_ref, s_ref, c_ref, m_ref, lw_ref, u_ref, attn_ref, kdec_ref, egl_ref, qk_ref, dl_ref):
            ref[...] = jnp.zeros_like(ref)
        for bi in range(gate_ref.shape[0]):
            gate_ref[bi, 0] = jnp.zeros((CHUNK, SMALL_W), F32)
            gate_ref[bi, 1] = jnp.full((CHUNK, SMALL_W), -jnp.inf, F32)
            gate_ref[bi, 2] = jnp.zeros((CHUNK, SMALL_W), F32)

    cw = convw_ref[...]
    gnorm = gnorm_ref[...]
    mnorm = mnorm_ref[...]
    ones3 = jnp.ones((CHUNK, 3 * CHUNK), BF16)
    consts = (negincl_ref[...], strict_ref[...], d16_ref[...], eye_ref[...], ones3, headsum_ref[...])
    lane = _lane_iota((CHUNK, SMALL_W))
    decay_lanes = (lane >= _L_DECAY) & (lane < _L_DECAY + HEADS)
    f_lanes = (lane >= _L_F) & (lane < _L_F + HEADS)
    neg_a = -jnp.exp(alog_ref[...])
    hq = HEADS * GDN_DK
    hm = HEADS * ML_DQK
    nb = gq_ref.shape[0]
    streams = []
    for bi in range(nb):
        streams.append(_gdn_scan(gz_ref[bi].astype(F32), gnorm, s_ref.at[bi], lw_ref.at[bi], u_ref.at[bi],
                                 attn_ref.at[bi], kdec_ref.at[bi], egl_ref.at[bi], oa_ref.at[bi]))
        mqk = mqkb_ref[bi].astype(F32)
        streams.append(_mlstm_scan(mqk[:, :hm], mqk[:, hm:], mv_ref[bi].astype(F32), mo_ref[bi].astype(F32),
                                   mnorm, c_ref.at[bi], m_ref.at[bi], qk_ref.at[bi], dl_ref.at[bi],
                                   gate_ref.at[bi], ob_ref.at[bi]))

    def prepare(bi):
        x = jnp.concatenate([gq_ref[bi], gk_ref[bi], gv_ref[bi]], axis=1).astype(F32)
        xbuf_ref[bi, HALO:, :] = x
        y = cw[CONV_WIDTH - 1:CONV_WIDTH, :] * x
        for tap in range(1, CONV_WIDTH):
            y = y + cw[CONV_WIDTH - 1 - tap:CONV_WIDTH - tap, :] * xbuf_ref[bi, HALO - tap:HALO - tap + CHUNK, :]
        xbuf_ref[bi, :HALO, :] = x[CHUNK - HALO:, :]
        qkv = _silu(y)

        t = sm_ref[bi] + bias_ref[...]
        sp = _softplus(t)
        beta_blk = _sigmoid(t)
        cs_in = jnp.where(f_lanes, t - sp, jnp.where(decay_lanes, neg_a * sp, 0.0))
        cs = jnp.dot(l3_ref[...], _split3_rows(cs_in), preferred_element_type=F32)
        yield
        mqk = mqka_ref[bi].astype(F32)
        yield from _round_robin([
            _gdn_prepare(qkv[:, :hq], qkv[:, hq:2 * hq], qkv[:, 2 * hq:], beta_blk, cs, consts,
                         lw_ref.at[bi], u_ref.at[bi], attn_ref.at[bi], kdec_ref.at[bi], egl_ref.at[bi]),
            _mlstm_prepare(mqk[:, :hm], mqk[:, hm:], t, cs, consts,
                           qk_ref.at[bi], dl_ref.at[bi], gate_ref.at[bi])])

    for _ in _round_robin([prepare(bi) for bi in range(nb)] + streams):
        pass


def _mixer_constants():
    i = np.arange(CHUNK)[:, None]
    j = np.tile(np.arange(CHUNK), HEADS)[None, :]
    negincl = np.where(i >= j, 0.0, -np.inf).astype(np.float32)
    strict = (i > j).astype(np.float32)
    d16 = ((i // SUB) == (j // SUB)).astype(np.float32)
    eye = (i == j).astype(np.float32)
    lower = (np.arange(CHUNK)[:, None] >= np.arange(CHUNK)[None, :]).astype(np.float32)
    l3 = np.tile(lower, (1, 3))
    headsum = (np.arange(2 * HEADS * GDN_DK)[:, None] // GDN_DK == np.arange(SMALL_W)[None, :]).astype(np.float32)
    return (jnp.asarray(negincl), jnp.asarray(strict), jnp.asarray(d16), jnp.asarray(eye),
            jnp.asarray(l3, dtype=BF16), jnp.asarray(headsum, dtype=BF16))


def _mixer(proj, small, conv_w, bias_row, alog_row, gnorm, mnorm, consts):
    _, bsz, seq, _ = proj.shape
    n_chunks = seq // CHUNK
    conv_ch = conv_w.shape[1]

    def prep(c):
        return jnp.minimum(c, n_chunks - 1)

    def scan(c):
        return jnp.maximum(c - 1, 0)

    def pblk(col, chunk_of):
        return pl.BlockSpec((None, bsz, CHUNK, COL), lambda c: (col, 0, chunk_of(c), 0))

    def const(a):
        return pl.BlockSpec(a.shape, lambda c: (0,) * a.ndim)

    pair = HEADS // 2
    fixed = (conv_w, bias_row, alog_row, gnorm, mnorm) + tuple(consts)
    return pl.pallas_call(
        _mixer_kernel,
        grid=(n_chunks + 1,),
        in_specs=[pblk(_GQ, prep), pblk(_GK, prep), pblk(_GV, prep), pblk(_MQK, prep),
                  pl.BlockSpec((bsz, CHUNK, SMALL_W), lambda c: (0, prep(c), 0)),
                  pblk(_GZ, scan), pblk(_MQK, scan), pblk(_MV, scan), pblk(_MO, scan)]
                 + [const(a) for a in fixed],
        out_specs=[pl.BlockSpec((bsz, CHUNK, HEADS * GDN_DV), lambda c: (0, scan(c), 0)),
                   pl.BlockSpec((bsz, CHUNK, HEADS * ML_DV), lambda c: (0, scan(c), 0))],
        out_shape=[jax.ShapeDtypeStruct((bsz, seq, HEADS * GDN_DV), BF16),
                   jax.ShapeDtypeStruct((bsz, seq, HEADS * ML_DV), BF16)],
        scratch_shapes=[
            pltpu.VMEM((bsz, HALO + CHUNK, conv_ch), F32),
            pltpu.VMEM((bsz, GDN_DK, HEADS * GDN_DV), F32),
            pltpu.VMEM((bsz, pair, 2 * ML_DQK, 2 * ML_DV), F32),
            pltpu.VMEM((bsz, 8, SMALL_W), F32),
            pltpu.VMEM((bsz, HEADS, 2 * CHUNK, GDN_DK), BF16),
            pltpu.VMEM((bsz, HEADS * CHUNK, GDN_DV), F32),
            pltpu.VMEM((bsz, pair, 2 * CHUNK, LANES), BF16),
            pltpu.VMEM((bsz, pair, 2 * CHUNK, GDN_DK), BF16),
            pltpu.VMEM((bsz, 8, SMALL_W), F32),
            pltpu.VMEM((bsz, CHUNK, PACKED), F32),
            pltpu.VMEM((bsz, CHUNK, PACKED), F32),
            pltpu.VMEM((bsz, 3, CHUNK, SMALL_W), F32),
        ],
        compiler_params=pltpu.CompilerParams(
            dimension_semantics=("arbitrary",), vmem_limit_bytes=VMEM_LIMIT),
        name="mixer",
    )(proj, proj, proj, proj, small, proj, proj, proj, proj, *fixed)


def _merge_kernel(oa_ref, ob_ref, ga0_ref, ga1_ref, gb0_ref, gb1_ref, x_ref, wa_ref, wb_ref, wo_ref, x1_ref):
    ya = jnp.dot(oa_ref[...], wa_ref[...], preferred_element_type=F32)
    yb = jnp.dot(ob_ref[...], wb_ref[...], preferred_element_type=F32)
    ga = jnp.concatenate([ga0_ref[...], ga1_ref[...]], axis=1).astype(F32)
    gb = jnp.concatenate([gb0_ref[...], gb1_ref[...]], axis=1).astype(F32)
    mixed = _sigmoid(ga) * ya + _sigmoid(gb) * yb
    x1_ref[...] = x_ref[...] + jnp.dot(mixed.astype(BF16), wo_ref[...], preferred_element_type=F32)


def _merge(oa, ob, proj, x, wa, wb, wo, tm):
    t, d = x.shape

    def full(a):
        return pl.BlockSpec(a.shape, lambda i: (0, 0))

    def gate(col):
        return pl.BlockSpec((None, tm, COL), lambda i: (col, i, 0))

    return pl.pallas_call(
        _merge_kernel,
        grid=(t // tm,),
        in_specs=[
            pl.BlockSpec((tm, oa.shape[1]), lambda i: (i, 0)),
            pl.BlockSpec((tm, ob.shape[1]), lambda i: (i, 0)),
            gate(_GA), gate(_GA + 1), gate(_GB), gate(_GB + 1),
            pl.BlockSpec((tm, d), lambda i: (i, 0)),
            full(wa), full(wb), full(wo),
        ],
        out_specs=pl.BlockSpec((tm, d), lambda i: (i, 0)),
        out_shape=jax.ShapeDtypeStruct((t, d), F32),
        compiler_params=pltpu.CompilerParams(
            dimension_semantics=("parallel",), vmem_limit_bytes=VMEM_LIMIT),
        name="merge",
    )(oa, ob, proj, proj, proj, proj, x, wa, wb, wo)


def _ffn_kernel(x1_ref, p_ref, w1_ref, w3_ref, w2_ref, gffn_ref, gple_ref, wpg_ref, wple_ref, gfin_ref,
                out_ref, acc_ref, h_ref, *, final):
    j = pl.program_id(1)

    @pl.when(j == 0)
    def _():
        h_ref[...] = _rms(x1_ref[...], gffn_ref[...]).astype(BF16)

    h = h_ref[...]
    a = jnp.dot(h, w1_ref[...], preferred_element_type=F32)
    b = jnp.dot(h, w3_ref[...], preferred_element_type=F32)
    part = jnp.dot((_silu(a) * b).astype(BF16), w2_ref[...], preferred_element_type=F32)

    @pl.when(j == 0)
    def _():
        acc_ref[...] = part

    @pl.when(j > 0)
    def _():
        acc_ref[...] += part

    @pl.when(j == pl.num_programs(1) - 1)
    def _():
        x2 = x1_ref[...] + acc_ref[...]
        gate = _sigmoid(jnp.dot(_rms(x2, gple_ref[...]).astype(BF16), wpg_ref[...], preferred_element_type=F32))
        x3 = x2 + gate * jnp.dot(p_ref[...].astype(BF16), wple_ref[...], preferred_element_type=F32)
        if final:
            x3 = _rms(x3, gfin_ref[...])
        out_ref[...] = x3


def _ffn(x1, p, w1, w3, w2, gffn, gple, wpg, wple, gfin, tm, n_tiles, final):
    t, d = x1.shape
    tf = w1.shape[1] // n_tiles

    def full(a):
        return pl.BlockSpec(a.shape, lambda i, j: (0, 0))

    return pl.pallas_call(
        functools.partial(_ffn_kernel, final=final),
        grid=(t // tm, n_tiles),
        in_specs=[
            pl.BlockSpec((tm, d), lambda i, j: (i, 0)),
            pl.BlockSpec((tm, p.shape[1]), lambda i, j: (i, 0)),
            pl.BlockSpec((d, tf), lambda i, j: (0, j)),
            pl.BlockSpec((d, tf), lambda i, j: (0, j)),
            pl.BlockSpec((tf, d), lambda i, j: (j, 0)),
            full(gffn), full(gple), full(wpg), full(wple), full(gfin),
        ],
        out_specs=pl.BlockSpec((tm, d), lambda i, j: (i, 0)),
        out_shape=jax.ShapeDtypeStruct((t, d), F32),
        scratch_shapes=[pltpu.VMEM((tm, d), F32), pltpu.VMEM((tm, d), BF16)],
        compiler_params=pltpu.CompilerParams(
            dimension_semantics=("parallel", "arbitrary"), vmem_limit_bytes=VMEM_LIMIT),
        name="ffn_final" if final else "ffn",
    )(x1, p, w1, w3, w2, gffn, gple, wpg, wple, gfin)


def _pack_w_in(w_in):
    sizes = (HEADS * GDN_DK, HEADS * GDN_DK, HEADS * GDN_DV, HEADS * GDN_DV, HEADS, HEADS,
             HEADS * ML_DQK, HEADS * ML_DQK, HEADS * ML_DV, HEADS * ML_DV, HEADS, HEADS)
    d = w_in.shape[1]
    sizes = sizes + (d, d)
    assert sum(sizes) == w_in.shape[2]
    offs = np.concatenate([[0], np.cumsum(sizes)])
    seg = [w_in[:, :, int(offs[k]):int(offs[k + 1])] for k in range(len(sizes))]
    gq, gk, gv, gz, gbeta, galpha, mq, mk, mv, mo, mi, mf, gate_a, gate_b = seg
    main = jnp.concatenate([gate_a, gate_b, gq, gk, gv, gz, mq, mk, mv, mo], axis=-1)
    pad = jnp.zeros(w_in.shape[:2] + (SMALL_W - 4 * HEADS,), w_in.dtype)
    small = jnp.concatenate([gbeta, galpha, mi, mf, pad], axis=-1)
    return main, small


def _gate_rows(values, lane0):
    return jnp.pad(values, ((0, 0), (lane0, SMALL_W - lane0 - values.shape[1])))[:, None, :]


def kernel(x, p, g_mix, w_in, conv_w, a_log, dt_bias, gdn_norm, ml_i_bias, ml_f_bias, ml_norm, w_branch_a,
           w_branch_b, w_out, g_ffn, w1, w3, w2, g_ple, w_ple_gate, w_ple, g_final):
    bsz, seq, d = x.shape
    depth = w_in.shape[0]
    t = bsz * seq
    assert w_in.shape[2] == PACK_W + 4 * HEADS and seq % CHUNK == 0

    w_main, w_small = _pack_w_in(w_in.astype(BF16))
    wa, wb, wo = w_branch_a.astype(BF16), w_branch_b.astype(BF16), w_out.astype(BF16)
    w1b, w3b, w2b = w1.astype(BF16), w3.astype(BF16), w2.astype(BF16)
    wpg, wple = w_ple_gate.astype(BF16), w_ple.astype(BF16)
    bias_rows = _gate_rows(dt_bias, _L_DECAY) + _gate_rows(ml_i_bias, _L_I) + _gate_rows(ml_f_bias, _L_F)
    alog_rows = _gate_rows(a_log, _L_DECAY)
    gnorm = gdn_norm[:, None, :]
    mnorm = ml_norm.reshape(depth, 1, HEADS * ML_DV)
    consts = _mixer_constants()

    tm_in, tm_merge, tm_ffn, ffn_tiles = min(512, t), min(512, t), min(512, t), 2
    xt = x.reshape(t, d)
    for i in range(depth):
        proj, small = _inproj(xt, g_mix[i][None, :], w_main[i], w_small[i], tm_in)
        oa, ob = _mixer(proj.reshape(-1, bsz, seq, COL), small.reshape(bsz, seq, SMALL_W), conv_w[i],
                        bias_rows[i], alog_rows[i], gnorm[i], mnorm[i], consts)
        x1 = _merge(oa.reshape(t, -1), ob.reshape(t, -1), proj, xt, wa[i], wb[i], wo[i], tm_merge)
        xt = _ffn(x1, p[i].reshape(t, -1), w1b[i], w3b[i], w2b[i], g_ffn[i][None, :], g_ple[i][None, :],
                  wpg[i], wple[i], g_final[None, :], tm_ffn, ffn_tiles, final=(i == depth - 1))
    return xt.reshape(bsz, seq, d)
```

```python
import functools

import jax
import jax.numpy as jnp
import numpy as np
from jax import lax
from jax.experimental import pallas as pl
from jax.experimental.pallas import tpu as pltpu

F32 = jnp.float32
BF16 = jnp.bfloat16

NORM_EPS = 1e-6
CHUNK = 64
HEADS = 4
GDN_DK = 128
GDN_DV = 128
ML_DQK = 64
ML_DV = 128
CONV_WIDTH = 4
LANES = 128
MXU_WIDTH = 256
PACKED = HEADS * CHUNK
SUB = 16
SMALL_W = 128
COL = 512
HALO = 8

_L_BETA, _L_DECAY, _L_I, _L_F = 0, HEADS, 2 * HEADS, 3 * HEADS

_GA, _GB, _GQ, _GK, _GV, _GZ, _MQK, _MV, _MO = 0, 2, 4, 5, 6, 7, 8, 9, 10
PACK_W = 11 * COL

VMEM_LIMIT = 56 * 1024 * 1024


def _mm(a, b):
    return jnp.dot(a.astype(BF16), b.astype(BF16), preferred_element_type=F32)


def _mm_nt(a, b):
    return lax.dot_general(a.astype(BF16), b.astype(BF16), (((1,), (1,)), ((), ())),
                           preferred_element_type=F32)


def _mm_tn(a, b):
    return lax.dot_general(a.astype(BF16), b.astype(BF16), (((0,), (0,)), ((), ())),
                           preferred_element_type=F32)


def _rms(x, gain):
    return x * lax.rsqrt(jnp.mean(x * x, axis=-1, keepdims=True) + NORM_EPS) * gain


def _sigmoid(x):
    return 0.5 * jnp.tanh(0.5 * x) + 0.5


def _silu(x):
    h = 0.5 * x
    return h * jnp.tanh(h) + h


def _softplus(x):
    return jnp.maximum(x, 0.0) + jnp.log(1.0 + jnp.exp(-jnp.abs(x)))


def _split3_rows(x):
    hi = x.astype(BF16)
    r = x - hi.astype(F32)
    mid = r.astype(BF16)
    lo = (r - mid.astype(F32)).astype(BF16)
    return jnp.concatenate([hi, mid, lo], axis=0)


def _inproj_kernel(x_ref, g_ref, w_ref, ws_ref, o_ref, os_ref):
    hb = _rms(x_ref[...], g_ref[...]).astype(BF16)
    os_ref[...] = jnp.dot(hb, ws_ref[...], preferred_element_type=F32)
    for k in range(o_ref.shape[0]):
        o_ref[k] = jnp.dot(hb, w_ref[:, COL * k:COL * (k + 1)], preferred_element_type=F32).astype(BF16)


def _inproj(x, gain, w_main, w_small, tm):
    t, d = x.shape
    n_col = w_main.shape[1] // COL
    return pl.pallas_call(
        _inproj_kernel,
        grid=(t // tm,),
        in_specs=[
            pl.BlockSpec((tm, d), lambda i: (i, 0)),
            pl.BlockSpec((1, d), lambda i: (0, 0)),
            pl.BlockSpec(w_main.shape, lambda i: (0, 0), pipeline_mode=pl.Buffered(1)),
            pl.BlockSpec((d, SMALL_W), lambda i: (0, 0), pipeline_mode=pl.Buffered(1)),
        ],
        out_specs=[
            pl.BlockSpec((n_col, tm, COL), lambda i: (0, i, 0)),
            pl.BlockSpec((tm, SMALL_W), lambda i: (i, 0)),
        ],
        out_shape=[
            jax.ShapeDtypeStruct((n_col, t, COL), BF16),
            jax.ShapeDtypeStruct((t, SMALL_W), F32),
        ],
        compiler_params=pltpu.CompilerParams(
            dimension_semantics=("parallel",), vmem_limit_bytes=VMEM_LIMIT),
        name="inproj",
    )(x, gain, w_main, w_small)


def _lane_iota(shape):
    return lax.broadcasted_iota(jnp.int32, shape, 1)


def _lane_head(shape):
    return lax.shift_right_logical(_lane_iota(shape), int(np.log2(CHUNK)))


def _expand(blk, lane0):
    low = _lane_iota((CHUNK, LANES)) < CHUNK
    cols = [jnp.broadcast_to(blk[:, lane0 + h:lane0 + h + 1], (CHUNK, LANES)) for h in range(HEADS)]
    return jnp.concatenate([jnp.where(low, cols[2 * p], cols[2 * p + 1]) for p in range(HEADS // 2)], axis=1)


def _row_form(colexp, eyep, ones3):
    return jnp.dot(ones3, _split3_rows(colexp * eyep), preferred_element_type=F32)


def _block_diag(b):
    bb = b.astype(BF16)
    head = _lane_head((CHUNK, PACKED))
    return jnp.concatenate([jnp.where(head == h, bb, jnp.zeros_like(bb)) for h in range(HEADS)], axis=0)


def _pprod(a, b):
    return jnp.dot(a.astype(BF16), _block_diag(b), preferred_element_type=F32)


def _pair_rows(a, p):
    blk = a[:, LANES * p:LANES * (p + 1)]
    low = _lane_iota(blk.shape) < CHUNK
    zero = jnp.zeros_like(blk)
    return jnp.concatenate([jnp.where(low, blk, zero), jnp.where(low, zero, blk)], axis=0)


def _head_cols(a, h, width=LANES):
    return a[:, width * h:width * (h + 1)]


def _round_robin(streams):
    streams = list(streams)
    while streams:
        alive = []
        for g in streams:
            try:
                next(g)
                alive.append(g)
            except StopIteration:
                pass
        streams = alive
        if streams:
            yield


def _gdn_prepare(q, k, v, beta_blk, cs, consts, lw_ref, u_ref, attn_ref, kdec_ref, egl_ref):
    neginclp, strictp, d16p, eyep, ones3, headsum = consts
    qh = [_head_cols(q, h) for h in range(HEADS)]
    kh = [_head_cols(k, h) for h in range(HEADS)]
    ssq = jnp.dot(jnp.concatenate([q * q, k * k], axis=1).astype(BF16), headsum, preferred_element_type=F32)
    egc_blk = jnp.exp(cs)
    gl_row = cs[CHUNK - 1:CHUNK, :]
    kdec_blk = jnp.exp(gl_row - cs)
    egl_row = jnp.exp(gl_row)
    gcol = _expand(cs, _L_DECAY)
    grow = _row_form(gcol, eyep, ones3)
    yield

    inv_norm = lax.rsqrt(ssq + NORM_EPS)
    qn, kn, kb, qg, vb, kbe, kdec = [], [], [], [], [], [], []
    for h in range(HEADS):
        qsc = inv_norm[:, h:h + 1] * (GDN_DK ** -0.5)
        ksc = inv_norm[:, HEADS + h:HEADS + h + 1]
        beta = beta_blk[:, _L_BETA + h:_L_BETA + h + 1]
        egc = egc_blk[:, _L_DECAY + h:_L_DECAY + h + 1]
        qn.append(qh[h] * qsc)
        kn.append(kh[h] * ksc)
        kb.append(kh[h] * (ksc * beta))
        qg.append(qh[h] * (qsc * egc))
        vb.append(_head_cols(v, h) * beta)
        kbe.append(kh[h] * (ksc * beta * egc))
        kdec.append(kh[h] * (ksc * kdec_blk[:, _L_DECAY + h:_L_DECAY + h + 1]))

    zero = jnp.zeros((CHUNK, LANES), BF16)
    kexp = jnp.concatenate(
        [jnp.concatenate([kn[h].astype(BF16) if g == h else zero for g in range(HEADS)], axis=1)
         for h in range(HEADS)], axis=0)
    lhs = jnp.concatenate([jnp.concatenate(qn, axis=1), jnp.concatenate(kb, axis=1)], axis=0)
    gram = _mm_nt(lhs, kexp)
    decay = jnp.exp(gcol - grow + neginclp)
    yield

    attn = gram[:CHUNK] * decay
    m = gram[CHUNK:] * decay * strictp

    md = m * d16p
    lo = m - md
    n1 = -md
    n2 = _pprod(n1, n1)
    yield
    dgi = eyep + n1
    n4 = _pprod(n2, n2)
    t = _pprod(n2, dgi)
    yield
    dgi = dgi + t
    n8 = _pprod(n4, n4)
    t = _pprod(n4, dgi)
    yield
    dgi = dgi + t
    t = _pprod(n8, dgi)
    yield
    dgi = dgi + t
    r = _pprod(dgi, lo)
    yield
    r2 = _pprod(r, r)
    yield
    t = _pprod(r2, dgi)
    yield
    w1 = dgi + t
    t = _pprod(r, w1)
    yield
    tinv = w1 - t

    pairs = [(2 * p, 2 * p + 1) for p in range(HEADS // 2)]
    sol = []
    for p, heads in enumerate(pairs):
        rhs = jnp.concatenate([jnp.concatenate([vb[h], kbe[h]], axis=1) for h in heads], axis=0)
        sol.append(_mm(_pair_rows(tinv, p), rhs))
    yield
    for h in range(HEADS):
        rows = slice(CHUNK * (h % 2), CHUNK * (h % 2 + 1))
        lw_ref[h] = jnp.concatenate([qg[h], sol[h // 2][rows, GDN_DV:]], axis=0).astype(BF16)
        u_ref[CHUNK * h:CHUNK * (h + 1), :] = sol[h // 2][rows, :GDN_DV]
    for p, heads in enumerate(pairs):
        attn_ref[p] = _pair_rows(attn, p).astype(BF16)
        kdec_ref[p] = jnp.concatenate([kdec[h] for h in heads], axis=0).astype(BF16)
    egl_ref[...] = jnp.broadcast_to(egl_row, egl_ref.shape)


def _gdn_scan(z, gnorm, s_ref, lw_ref, u_ref, attn_ref, kdec_ref, egl_ref, o_ref):
    pairs = [(2 * p, 2 * p + 1) for p in range(HEADS // 2)]
    egl_row = egl_ref[0:1, :]
    ph = [jnp.dot(lw_ref[h], s_ref[:, GDN_DV * h:GDN_DV * (h + 1)].astype(BF16), preferred_element_type=F32)
          for h in range(HEADS)]
    yield
    vnew = [u_ref[CHUNK * h:CHUNK * (h + 1), :] - ph[h][CHUNK:] for h in range(HEADS)]
    intra, upd = [], []
    for p, heads in enumerate(pairs):
        v0, v1 = vnew[heads[0]], vnew[heads[1]]
        intra.append(_mm(attn_ref[p], jnp.concatenate([v0, v1], axis=0)))
        zv = jnp.zeros_like(v0)
        vbd = jnp.concatenate([jnp.concatenate([v0, zv], axis=1), jnp.concatenate([zv, v1], axis=1)], axis=0)
        upd.append(_mm_tn(kdec_ref[p], vbd))
    yield
    out = []
    for p, heads in enumerate(pairs):
        egl = jnp.concatenate(
            [jnp.broadcast_to(egl_row[:, _L_DECAY + h:_L_DECAY + h + 1], (GDN_DK, GDN_DV)) for h in heads], axis=1)
        cols = slice(2 * GDN_DV * p, 2 * GDN_DV * (p + 1))
        s_ref[:, cols] = s_ref[:, cols] * egl + upd[p]
        for idx, h in enumerate(heads):
            out.append(ph[h][:CHUNK] + intra[p][CHUNK * idx:CHUNK * (idx + 1)])
    oss = [jnp.mean(o * o, axis=-1, keepdims=True) for o in out]
    yield
    out = [out[h] * lax.rsqrt(oss[h] + NORM_EPS) * gnorm * _silu(_head_cols(z, h)) for h in range(HEADS)]
    o_ref[...] = jnp.concatenate(out, axis=1).astype(BF16)


def _mlstm_prepare(q, k, gates, cs, consts, qk_ref, dl_ref, gate_ref):
    neginclp, _, _, eyep, ones3, _ = consts
    head = _lane_head((CHUNK, PACKED))
    kb = k.astype(BF16)
    kexp = jnp.concatenate([jnp.where(head == h, kb, jnp.zeros_like(kb)) for h in range(HEADS)], axis=0)
    qk = _mm_nt(q * (ML_DQK ** -0.5), kexp)
    bcol = _expand(cs, _L_F)
    crow = _row_form(_expand(gates, _L_I) - bcol, eyep, ones3)
    yield

    dl = bcol + crow + neginclp
    lane = _lane_iota((CHUNK, LANES))
    low = lane < CHUNK
    rowmax = jnp.zeros((CHUNK, LANES), F32)
    for h in range(HEADS):
        blk = _head_cols(dl, h // 2)
        mx = jnp.max(jnp.where(low if h % 2 == 0 else ~low, blk, -jnp.inf), axis=-1, keepdims=True)
        rowmax = jnp.where(lane == _L_F + h, mx, rowmax)
    qk_ref[...] = qk
    dl_ref[...] = dl
    gate_ref[0] = cs
    gate_ref[1] = cs[CHUNK - 1:CHUNK, :] - cs + pltpu.roll(gates, _L_F - _L_I, 1)
    gate_ref[2] = rowmax


def _mlstm_scan(q, k, v, og, mnorm, c_ref, m_ref, qk_ref, dl_ref, gate_ref, o_ref):
    cs, a, rowmax = gate_ref[0], gate_ref[1], gate_ref[2]
    m_row = m_ref[0:1, :]
    m_inter = cs + m_row
    m_t = jnp.maximum(m_inter, rowmax)
    w_inter = jnp.exp(m_inter - m_t)
    emt = jnp.exp(-m_t)
    s = qk_ref[...] * jnp.exp(dl_ref[...] - _expand(m_t, _L_F))
    qw = q * (ML_DQK ** -0.5) * _expand(w_inter, _L_F)

    b_last = cs[CHUNK - 1:CHUNK, :]
    m_new = jnp.maximum(b_last + m_row, jnp.max(a, axis=0, keepdims=True))
    scale_row = jnp.exp(b_last + m_row - m_new)
    kw = k * _expand(jnp.exp(a - m_new), _L_F)
    m_ref[...] = jnp.broadcast_to(m_new, m_ref.shape)

    ones_col = (_lane_iota((CHUNK, ML_DV)) == 0).astype(F32)
    pairs = [(2 * p, 2 * p + 1) for p in range(HEADS // 2)]
    acc, upd = [], []
    for p, heads in enumerate(pairs):
        vext = jnp.concatenate(
            [jnp.concatenate([_head_cols(v, h), ones_col], axis=1) for h in heads], axis=0)
        lhs = jnp.concatenate([_pair_rows(qw, p), _pair_rows(s, p)], axis=1)
        acc.append(_mm(lhs, jnp.concatenate([c_ref[p], vext], axis=0)))
        upd.append(_mm_tn(_pair_rows(kw, p), vext))
    yield
    out = []
    for p, heads in enumerate(pairs):
        scale = jnp.concatenate(
            [jnp.broadcast_to(scale_row[:, _L_F + h:_L_F + h + 1], (ML_DQK, 2 * ML_DV)) for h in heads], axis=0)
        c_ref[p] = c_ref[p] * scale + upd[p]
    den = []
    for h in range(HEADS):
        rows = slice(CHUNK * (h % 2), CHUNK * (h % 2 + 1))
        d = jnp.maximum(jnp.abs(acc[h // 2][rows, ML_DV:ML_DV + 1]), emt[:, _L_F + h:_L_F + h + 1])
        den.append(jnp.broadcast_to(d, (CHUNK, ML_DV)))
    yield
    out = [acc[h // 2][CHUNK * (h % 2):CHUNK * (h % 2 + 1), :ML_DV] / den[h] for h in range(HEADS)]
    oss = [jnp.mean(o * o, axis=-1, keepdims=True) for o in out]
    yield
    out = [out[h] * lax.rsqrt(oss[h] + NORM_EPS) * _head_cols(mnorm, h) * _sigmoid(_head_cols(og, h))
           for h in range(HEADS)]
    o_ref[...] = jnp.concatenate(out, axis=1).astype(BF16)


def _mixer_kernel(gq_ref, gk_ref, gv_ref, mqka_ref, sm_ref, gz_ref, mqkb_ref, mv_ref, mo_ref,
                  convw_ref, bias_ref, alog_ref, gnorm_ref, mnorm_ref,
                  negincl_ref, strict_ref, d16_ref, eye_ref, l3_ref, headsum_ref,
                  oa_ref, ob_ref, xbuf_ref, s_ref, c_ref, m_ref,
                  lw_ref, u_ref, attn_ref, kdec_ref, egl_ref, qk_ref, dl_ref, gate_ref):
    @pl.when(pl.program_id(0) == 0)
    def _():
        for ref in (xbuf_ref, s_ref, c_ref, m_ref, lw_ref, u_ref, attn_ref, kdec_ref, egl_ref, qk_ref, dl_ref):
            ref[...] = jnp.zeros_like(ref)
        for bi in range(gate_ref.shape[0]):
            gate_ref[bi, 0] = jnp.zeros((CHUNK, SMALL_W), F32)
            gate_ref[bi, 1] = jnp.full((CHUNK, SMALL_W), -jnp.inf, F32)
            gate_ref[bi, 2] = jnp.zeros((CHUNK, SMALL_W), F32)

    cw = convw_ref[...]
    gnorm = gnorm_ref[...]
    mnorm = mnorm_ref[...]
    ones3 = jnp.ones((CHUNK, 3 * CHUNK), BF16)
    consts = (negincl_ref[...], strict_ref[...], d16_ref[...], eye_ref[...], ones3, headsum_ref[...])
    lane = _lane_iota((CHUNK, SMALL_W))
    decay_lanes = (lane >= _L_DECAY) & (lane < _L_DECAY + HEADS)
    f_lanes = (lane >= _L_F) & (lane < _L_F + HEADS)
    neg_a = -jnp.exp(alog_ref[...])
    hq = HEADS * GDN_DK
    hm = HEADS * ML_DQK
    nb = gq_ref.shape[0]
    streams = []
    for bi in range(nb):
        streams.append(_gdn_scan(gz_ref[bi].astype(F32), gnorm, s_ref.at[bi], lw_ref.at[bi], u_ref.at[bi],
                                 attn_ref.at[bi], kdec_ref.at[bi], egl_ref.at[bi], oa_ref.at[bi]))
        mqk = mqkb_ref[bi].astype(F32)
        streams.append(_mlstm_scan(mqk[:, :hm], mqk[:, hm:], mv_ref[bi].astype(F32), mo_ref[bi].astype(F32),
                                   mnorm, c_ref.at[bi], m_ref.at[bi], qk_ref.at[bi], dl_ref.at[bi],
                                   gate_ref.at[bi], ob_ref.at[bi]))

    def prepare(bi):
        x = jnp.concatenate([gq_ref[bi], gk_ref[bi], gv_ref[bi]], axis=1).astype(F32)
        xbuf_ref[bi, HALO:, :] = x
        y = cw[CONV_WIDTH - 1:CONV_WIDTH, :] * x
        for tap in range(1, CONV_WIDTH):
            y = y + cw[CONV_WIDTH - 1 - tap:CONV_WIDTH - tap, :] * xbuf_ref[bi, HALO - tap:HALO - tap + CHUNK, :]
        xbuf_ref[bi, :HALO, :] = x[CHUNK - HALO:, :]
        qkv = _silu(y)

        t = sm_ref[bi] + bias_ref[...]
        sp = _softplus(t)
        beta_blk = _sigmoid(t)
        cs_in = jnp.where(f_lanes, t - sp, jnp.where(decay_lanes, neg_a * sp, 0.0))
        cs = jnp.dot(l3_ref[...], _split3_rows(cs_in), preferred_element_type=F32)
        yield
        mqk = mqka_ref[bi].astype(F32)
        yield from _round_robin([
            _gdn_prepare(qkv[:, :hq], qkv[:, hq:2 * hq], qkv[:, 2 * hq:], beta_blk, cs, consts,
                         lw_ref.at[bi], u_ref.at[bi], attn_ref.at[bi], kdec_ref.at[bi], egl_ref.at[bi]),
            _mlstm_prepare(mqk[:, :hm], mqk[:, hm:], t, cs, consts,
                           qk_ref.at[bi], dl_ref.at[bi], gate_ref.at[bi])])

    for _ in _round_robin([prepare(bi) for bi in range(nb)] + streams):
        pass


def _mixer_constants():
    i = np.arange(CHUNK)[:, None]
    j = np.tile(np.arange(CHUNK), HEADS)[None, :]
    negincl = np.where(i >= j, 0.0, -np.inf).astype(np.float32)
    strict = (i > j).astype(np.float32)
    d16 = ((i // SUB) == (j // SUB)).astype(np.float32)
    eye = (i == j).astype(np.float32)
    lower = (np.arange(CHUNK)[:, None] >= np.arange(CHUNK)[None, :]).astype(np.float32)
    l3 = np.tile(lower, (1, 3))
    headsum = (np.arange(2 * HEADS * GDN_DK)[:, None] // GDN_DK == np.arange(SMALL_W)[None, :]).astype(np.float32)
    return (jnp.asarray(negincl), jnp.asarray(strict), jnp.asarray(d16), jnp.asarray(eye),
            jnp.asarray(l3, dtype=BF16), jnp.asarray(headsum, dtype=BF16))


def _mixer(proj, small, conv_w, bias_row, alog_row, gnorm, mnorm, consts):
    _, bsz, seq, _ = proj.shape
    n_chunks = seq // CHUNK
    conv_ch = conv_w.shape[1]

    def prep(c):
        return jnp.minimum(c, n_chunks - 1)

    def scan(c):
        return jnp.maximum(c - 1, 0)

    def pblk(col, chunk_of):
        return pl.BlockSpec((None, bsz, CHUNK, COL), lambda c: (col, 0, chunk_of(c), 0))

    def const(a):
        return pl.BlockSpec(a.shape, lambda c: (0,) * a.ndim)

    pair = HEADS // 2
    fixed = (conv_w, bias_row, alog_row, gnorm, mnorm) + tuple(consts)
    return pl.pallas_call(
        _mixer_kernel,
        grid=(n_chunks + 1,),
        in_specs=[pblk(_GQ, prep), pblk(_GK, prep), pblk(_GV, prep), pblk(_MQK, prep),
                  pl.BlockSpec((bsz, CHUNK, SMALL_W), lambda c: (0, prep(c), 0)),
                  pblk(_GZ, scan), pblk(_MQK, scan), pblk(_MV, scan), pblk(_MO, scan)]
                 + [const(a) for a in fixed],
        out_specs=[pl.BlockSpec((bsz, CHUNK, HEADS * GDN_DV), lambda c: (0, scan(c), 0)),
                   pl.BlockSpec((bsz, CHUNK, HEADS * ML_DV), lambda c: (0, scan(c), 0))],
        out_shape=[jax.ShapeDtypeStruct((bsz, seq, HEADS * GDN_DV), BF16),
                   jax.ShapeDtypeStruct((bsz, seq, HEADS * ML_DV), BF16)],
        scratch_shapes=[
            pltpu.VMEM((bsz, HALO + CHUNK, conv_ch), F32),
            pltpu.VMEM((bsz, GDN_DK, HEADS * GDN_DV), F32),
            pltpu.VMEM((bsz, pair, 2 * ML_DQK, 2 * ML_DV), F32),
            pltpu.VMEM((bsz, 8, SMALL_W), F32),
            pltpu.VMEM((bsz, HEADS, 2 * CHUNK, GDN_DK), BF16),
            pltpu.VMEM((bsz, HEADS * CHUNK, GDN_DV), F32),
            pltpu.VMEM((bsz, pair, 2 * CHUNK, LANES), BF16),
            pltpu.VMEM((bsz, pair, 2 * CHUNK, GDN_DK), BF16),
            pltpu.VMEM((bsz, 8, SMALL_W), F32),
            pltpu.VMEM((bsz, CHUNK, PACKED), F32),
            pltpu.VMEM((bsz, CHUNK, PACKED), F32),
            pltpu.VMEM((bsz, 3, CHUNK, SMALL_W), F32),
        ],
        compiler_params=pltpu.CompilerParams(
            dimension_semantics=("arbitrary",), vmem_limit_bytes=VMEM_LIMIT),
        name="mixer",
    )(proj, proj, proj, proj, small, proj, proj, proj, proj, *fixed)


def _tail_kernel(oa_ref, ob_ref, ga0_ref, ga1_ref, gb0_ref, gb1_ref, x_ref, p_ref,
                 wa_ref, wb_ref, wo_ref, w1_ref, w3_ref, w2_ref, wpg_ref, wple_ref,
                 gffn_ref, gple_ref, gfin_ref, out_ref, *, final, splits):
    ya = jnp.dot(oa_ref[...], wa_ref[...], preferred_element_type=F32)
    yb = jnp.dot(ob_ref[...], wb_ref[...], preferred_element_type=F32)
    ga = jnp.concatenate([ga0_ref[...], ga1_ref[...]], axis=1).astype(F32)
    gb = jnp.concatenate([gb0_ref[...], gb1_ref[...]], axis=1).astype(F32)
    mixed = _sigmoid(ga) * ya + _sigmoid(gb) * yb
    x1 = x_ref[...] + jnp.dot(mixed.astype(BF16), wo_ref[...], preferred_element_type=F32)

    h = _rms(x1, gffn_ref[...]).astype(BF16)
    x2 = x1
    for lo, hi in splits:
        a = jnp.dot(h, w1_ref[:, lo:hi], preferred_element_type=F32)
        b = jnp.dot(h, w3_ref[:, lo:hi], preferred_element_type=F32)
        x2 = x2 + jnp.dot((_silu(a) * b).astype(BF16), w2_ref[lo:hi, :], preferred_element_type=F32)

    gate = _sigmoid(jnp.dot(_rms(x2, gple_ref[...]).astype(BF16), wpg_ref[...], preferred_element_type=F32))
    x3 = x2 + gate * jnp.dot(p_ref[...].astype(BF16), wple_ref[...], preferred_element_type=F32)
    if final:
        x3 = _rms(x3, gfin_ref[...])
    out_ref[...] = x3


def _tail(oa, ob, proj, x, p, weights, gains, tm, final):
    t, d = x.shape
    dff = weights[3].shape[1]
    half = (dff // MXU_WIDTH // 2) * MXU_WIDTH
    splits = ((0, half), (half, dff))

    def rows(width):
        return pl.BlockSpec((tm, width), lambda i: (i, 0))

    def gate(col):
        return pl.BlockSpec((None, tm, COL), lambda i: (col, i, 0))

    def resident(a):
        return pl.BlockSpec(a.shape, lambda i: (0, 0), pipeline_mode=pl.Buffered(1))

    return pl.pallas_call(
        functools.partial(_tail_kernel, final=final, splits=splits),
        grid=(t // tm,),
        in_specs=[rows(oa.shape[1]), rows(ob.shape[1]), gate(_GA), gate(_GA + 1), gate(_GB), gate(_GB + 1),
                  rows(d), rows(p.shape[1])] + [resident(a) for a in weights + gains],
        out_specs=rows(d),
        out_shape=jax.ShapeDtypeStruct((t, d), F32),
        compiler_params=pltpu.CompilerParams(
            dimension_semantics=("parallel",), vmem_limit_bytes=VMEM_LIMIT),
        name="tail_final" if final else "tail",
    )(oa, ob, proj, proj, proj, proj, x, p, *weights, *gains)


def _pack_w_in(w_in):
    sizes = (HEADS * GDN_DK, HEADS * GDN_DK, HEADS * GDN_DV, HEADS * GDN_DV, HEADS, HEADS,
             HEADS * ML_DQK, HEADS * ML_DQK, HEADS * ML_DV, HEADS * ML_DV, HEADS, HEADS)
    d = w_in.shape[1]
    sizes = sizes + (d, d)
    assert sum(sizes) == w_in.shape[2]
    offs = np.concatenate([[0], np.cumsum(sizes)])
    seg = [w_in[:, :, int(offs[k]):int(offs[k + 1])] for k in range(len(sizes))]
    gq, gk, gv, gz, gbeta, galpha, mq, mk, mv, mo, mi, mf, gate_a, gate_b = seg
    main = jnp.concatenate([gate_a, gate_b, gq, gk, gv, gz, mq, mk, mv, mo], axis=-1)
    pad = jnp.zeros(w_in.shape[:2] + (SMALL_W - 4 * HEADS,), w_in.dtype)
    small = jnp.concatenate([gbeta, galpha, mi, mf, pad], axis=-1)
    return main, small


def _gate_rows(values, lane0):
    return jnp.pad(values, ((0, 0), (lane0, SMALL_W - lane0 - values.shape[1])))[:, None, :]


def kernel(x, p, g_mix, w_in, conv_w, a_log, dt_bias, gdn_norm, ml_i_bias, ml_f_bias, ml_norm, w_branch_a,
           w_branch_b, w_out, g_ffn, w1, w3, w2, g_ple, w_ple_gate, w_ple, g_final):
    bsz, seq, d = x.shape
    depth = w_in.shape[0]
    t = bsz * seq
    assert w_in.shape[2] == PACK_W + 4 * HEADS and seq % CHUNK == 0

    w_main, w_small = _pack_w_in(w_in.astype(BF16))
    wa, wb, wo = w_branch_a.astype(BF16), w_branch_b.astype(BF16), w_out.astype(BF16)
    w1b, w3b, w2b = w1.astype(BF16), w3.astype(BF16), w2.astype(BF16)
    wpg, wple = w_ple_gate.astype(BF16), w_ple.astype(BF16)
    bias_rows = _gate_rows(dt_bias, _L_DECAY) + _gate_rows(ml_i_bias, _L_I) + _gate_rows(ml_f_bias, _L_F)
    alog_rows = _gate_rows(a_log, _L_DECAY)
    gnorm = gdn_norm[:, None, :]
    mnorm = ml_norm.reshape(depth, 1, HEADS * ML_DV)
    consts = _mixer_constants()

    tm_in, tm_tail = min(512, t), min(512, t)
    xt = x.reshape(t, d)
    for i in range(depth):
        proj, small = _inproj(xt, g_mix[i][None, :], w_main[i], w_small[i], tm_in)
        oa, ob = _mixer(proj.reshape(-1, bsz, seq, COL), small.reshape(bsz, seq, SMALL_W), conv_w[i],
                        bias_rows[i], alog_rows[i], gnorm[i], mnorm[i], consts)
        weights = (wa[i], wb[i], wo[i], w1b[i], w3b[i], w2b[i], wpg[i], wple[i])
        gains = (g_ffn[i][None, :], g_ple[i][None, :], g_final[None, :])
        xt = _tail(oa.reshape(t, -1), ob.reshape(t, -1), proj, xt, p[i].reshape(t, -1), weights, gains,
                   tm_tail, final=(i == depth - 1))
    return xt.reshape(bsz, seq, d)
```

```python
import functools

import jax
import jax.numpy as jnp
import numpy as np
from jax import lax
from jax.experimental import pallas as pl
from jax.experimental.pallas import tpu as pltpu

F32 = jnp.float32
BF16 = jnp.bfloat16

NORM_EPS = 1e-6
CHUNK = 64
HEADS = 4
GDN_DK = 128
GDN_DV = 128
ML_DQK = 64
ML_DV = 128
CONV_WIDTH = 4
LANES = 128
MXU_WIDTH = 256
PACKED = HEADS * CHUNK
SUB = 16
SMALL_W = 128
COL = 512
HALO = 8

_L_BETA, _L_DECAY, _L_I, _L_F = 0, HEADS, 2 * HEADS, 3 * HEADS

_GA, _GB, _GQ, _GK, _GV, _GZ, _MQK, _MV, _MO = 0, 2, 4, 5, 6, 7, 8, 9, 10
PACK_W = 11 * COL

VMEM_LIMIT = 56 * 1024 * 1024


def _mm(a, b):
    return jnp.dot(a.astype(BF16), b.astype(BF16), preferred_element_type=F32)


def _mm_nt(a, b):
    return lax.dot_general(a.astype(BF16), b.astype(BF16), (((1,), (1,)), ((), ())),
                           preferred_element_type=F32)


def _mm_tn(a, b):
    return lax.dot_general(a.astype(BF16), b.astype(BF16), (((0,), (0,)), ((), ())),
                           preferred_element_type=F32)


def _rms(x, gain):
    return x * lax.rsqrt(jnp.mean(x * x, axis=-1, keepdims=True) + NORM_EPS) * gain


def _sigmoid(x):
    return 0.5 * jnp.tanh(0.5 * x) + 0.5


def _silu(x):
    h = 0.5 * x
    return h * jnp.tanh(h) + h


def _softplus(x):
    return jnp.maximum(x, 0.0) + jnp.log(1.0 + jnp.exp(-jnp.abs(x)))


def _split3_rows(x):
    hi = x.astype(BF16)
    r = x - hi.astype(F32)
    mid = r.astype(BF16)
    lo = (r - mid.astype(F32)).astype(BF16)
    return jnp.concatenate([hi, mid, lo], axis=0)


def _inproj_kernel(x_ref, g_ref, w_ref, ws_ref, o_ref, os_ref):
    hb = _rms(x_ref[...], g_ref[...]).astype(BF16)
    os_ref[...] = jnp.dot(hb, ws_ref[...], preferred_element_type=F32)
    for k in range(o_ref.shape[0]):
        o_ref[k] = jnp.dot(hb, w_ref[:, COL * k:COL * (k + 1)], preferred_element_type=F32).astype(BF16)


def _inproj(x, gain, w_main, w_small, tm):
    t, d = x.shape
    n_col = w_main.shape[1] // COL
    return pl.pallas_call(
        _inproj_kernel,
        grid=(t // tm,),
        in_specs=[
            pl.BlockSpec((tm, d), lambda i: (i, 0)),
            pl.BlockSpec((1, d), lambda i: (0, 0)),
            pl.BlockSpec(w_main.shape, lambda i: (0, 0), pipeline_mode=pl.Buffered(1)),
            pl.BlockSpec((d, SMALL_W), lambda i: (0, 0), pipeline_mode=pl.Buffered(1)),
        ],
        out_specs=[
            pl.BlockSpec((n_col, tm, COL), lambda i: (0, i, 0)),
            pl.BlockSpec((tm, SMALL_W), lambda i: (i, 0)),
        ],
        out_shape=[
            jax.ShapeDtypeStruct((n_col, t, COL), BF16),
            jax.ShapeDtypeStruct((t, SMALL_W), F32),
        ],
        compiler_params=pltpu.CompilerParams(
            dimension_semantics=("parallel",), vmem_limit_bytes=VMEM_LIMIT),
        name="inproj",
    )(x, gain, w_main, w_small)


def _lane_iota(shape):
    return lax.broadcasted_iota(jnp.int32, shape, 1)


def _lane_head(shape):
    return lax.shift_right_logical(_lane_iota(shape), int(np.log2(CHUNK)))


def _expand(blk, lane0):
    low = _lane_iota((CHUNK, LANES)) < CHUNK
    cols = [jnp.broadcast_to(blk[:, lane0 + h:lane0 + h + 1], (CHUNK, LANES)) for h in range(HEADS)]
    return jnp.concatenate([jnp.where(low, cols[2 * p], cols[2 * p + 1]) for p in range(HEADS // 2)], axis=1)


def _row_form(colexp, eyep, ones3):
    return jnp.dot(ones3, _split3_rows(colexp * eyep), preferred_element_type=F32)


def _block_diag(b):
    bb = b.astype(BF16)
    head = _lane_head((CHUNK, PACKED))
    return jnp.concatenate([jnp.where(head == h, bb, jnp.zeros_like(bb)) for h in range(HEADS)], axis=0)


def _pprod(a, b):
    return jnp.dot(a.astype(BF16), _block_diag(b), preferred_element_type=F32)


def _pair_rows(a, p):
    blk = a[:, LANES * p:LANES * (p + 1)]
    low = _lane_iota(blk.shape) < CHUNK
    zero = jnp.zeros_like(blk)
    return jnp.concatenate([jnp.where(low, blk, zero), jnp.where(low, zero, blk)], axis=0)


def _head_cols(a, h, width=LANES):
    return a[:, width * h:width * (h + 1)]


def _round_robin(streams):
    streams = list(streams)
    while streams:
        alive = []
        for g in streams:
            try:
                next(g)
                alive.append(g)
            except StopIteration:
                pass
        streams = alive
        if streams:
            yield


def _gdn_prepare(q, k, v, beta_blk, cs, consts, lw_ref, u_ref, attn_ref, kdec_ref, egl_ref):
    neginclp, strictp, d16p, eyep, ones3, headsum = consts
    qh = [_head_cols(q, h) for h in range(HEADS)]
    kh = [_head_cols(k, h) for h in range(HEADS)]
    ssq = jnp.dot(jnp.concatenate([q * q, k * k], axis=1).astype(BF16), headsum, preferred_element_type=F32)
    egc_blk = jnp.exp(cs)
    gl_row = cs[CHUNK - 1:CHUNK, :]
    kdec_blk = jnp.exp(gl_row - cs)
    egl_row = jnp.exp(gl_row)
    gcol = _expand(cs, _L_DECAY)
    grow = _row_form(gcol, eyep, ones3)
    yield

    inv_norm = lax.rsqrt(ssq + NORM_EPS)
    qn, kn, kb, qg, vb, kbe, kdec = [], [], [], [], [], [], []
    for h in range(HEADS):
        qsc = inv_norm[:, h:h + 1] * (GDN_DK ** -0.5)
        ksc = inv_norm[:, HEADS + h:HEADS + h + 1]
        beta = beta_blk[:, _L_BETA + h:_L_BETA + h + 1]
        egc = egc_blk[:, _L_DECAY + h:_L_DECAY + h + 1]
        qn.append(qh[h] * qsc)
        kn.append(kh[h] * ksc)
        kb.append(kh[h] * (ksc * beta))
        qg.append(qh[h] * (qsc * egc))
        vb.append(_head_cols(v, h) * beta)
        kbe.append(kh[h] * (ksc * beta * egc))
        kdec.append(kh[h] * (ksc * kdec_blk[:, _L_DECAY + h:_L_DECAY + h + 1]))

    zero = jnp.zeros((CHUNK, LANES), BF16)
    kexp = jnp.concatenate(
        [jnp.concatenate([kn[h].astype(BF16) if g == h else zero for g in range(HEADS)], axis=1)
         for h in range(HEADS)], axis=0)
    lhs = jnp.concatenate([jnp.concatenate(qn, axis=1), jnp.concatenate(kb, axis=1)], axis=0)
    gram = _mm_nt(lhs, kexp)
    decay = jnp.exp(gcol - grow + neginclp)
    yield

    attn = gram[:CHUNK] * decay
    m = gram[CHUNK:] * decay * strictp

    md = m * d16p
    lo = m - md
    n1 = -md
    n2 = _pprod(n1, n1)
    yield
    dgi = eyep + n1
    n4 = _pprod(n2, n2)
    t = _pprod(n2, dgi)
    yield
    dgi = dgi + t
    n8 = _pprod(n4, n4)
    t = _pprod(n4, dgi)
    yield
    dgi = dgi + t
    t = _pprod(n8, dgi)
    yield
    dgi = dgi + t
    r = _pprod(dgi, lo)
    yield
    r2 = _pprod(r, r)
    yield
    t = _pprod(r2, dgi)
    yield
    w1 = dgi + t
    t = _pprod(r, w1)
    yield
    tinv = w1 - t

    pairs = [(2 * p, 2 * p + 1) for p in range(HEADS // 2)]
    sol = []
    for p, heads in enumerate(pairs):
        rhs = jnp.concatenate([jnp.concatenate([vb[h], kbe[h]], axis=1) for h in heads], axis=0)
        sol.append(_mm(_pair_rows(tinv, p), rhs))
    yield
    for h in range(HEADS):
        rows = slice(CHUNK * (h % 2), CHUNK * (h % 2 + 1))
        lw_ref[h] = jnp.concatenate([qg[h], sol[h // 2][rows, GDN_DV:]], axis=0).astype(BF16)
        u_ref[CHUNK * h:CHUNK * (h + 1), :] = sol[h // 2][rows, :GDN_DV]
    for p, heads in enumerate(pairs):
        attn_ref[p] = _pair_rows(attn, p).astype(BF16)
        kdec_ref[p] = jnp.concatenate([kdec[h] for h in heads], axis=0).astype(BF16)
    egl_ref[...] = jnp.broadcast_to(egl_row, egl_ref.shape)


def _gdn_scan(z, gnorm, s_ref, lw_ref, u_ref, attn_ref, kdec_ref, egl_ref, o_ref):
    pairs = [(2 * p, 2 * p + 1) for p in range(HEADS // 2)]
    egl_row = egl_ref[0:1, :]
    ph = [jnp.dot(lw_ref[h], s_ref[:, GDN_DV * h:GDN_DV * (h + 1)].astype(BF16), preferred_element_type=F32)
          for h in range(HEADS)]
    yield
    vnew = [u_ref[CHUNK * h:CHUNK * (h + 1), :] - ph[h][CHUNK:] for h in range(HEADS)]
    intra, upd = [], []
    for p, heads in enumerate(pairs):
        v0, v1 = vnew[heads[0]], vnew[heads[1]]
        intra.append(_mm(attn_ref[p], jnp.concatenate([v0, v1], axis=0)))
        zv = jnp.zeros_like(v0)
        vbd = jnp.concatenate([jnp.concatenate([v0, zv], axis=1), jnp.concatenate([zv, v1], axis=1)], axis=0)
        upd.append(_mm_tn(kdec_ref[p], vbd))
    yield
    out = []
    for p, heads in enumerate(pairs):
        egl = jnp.concatenate(
            [jnp.broadcast_to(egl_row[:, _L_DECAY + h:_L_DECAY + h + 1], (GDN_DK, GDN_DV)) for h in heads], axis=1)
        cols = slice(2 * GDN_DV * p, 2 * GDN_DV * (p + 1))
        s_ref[:, cols] = s_ref[:, cols] * egl + upd[p]
        for idx, h in enumerate(heads):
            out.append(ph[h][:CHUNK] + intra[p][CHUNK * idx:CHUNK * (idx + 1)])
    oss = [jnp.mean(o * o, axis=-1, keepdims=True) for o in out]
    yield
    out = [out[h] * lax.rsqrt(oss[h] + NORM_EPS) * gnorm * _silu(_head_cols(z, h)) for h in range(HEADS)]
    o_ref[...] = jnp.concatenate(out, axis=1).astype(BF16)


def _mlstm_prepare(q, k, gates, cs, consts, qk_ref, dl_ref, gate_ref):
    neginclp, _, _, eyep, ones3, _ = consts
    head = _lane_head((CHUNK, PACKED))
    kb = k.astype(BF16)
    kexp = jnp.concatenate([jnp.where(head == h, kb, jnp.zeros_like(kb)) for h in range(HEADS)], axis=0)
    qk = _mm_nt(q * (ML_DQK ** -0.5), kexp)
    bcol = _expand(cs, _L_F)
    crow = _row_form(_expand(gates, _L_I) - bcol, eyep, ones3)
    yield

    dl = bcol + crow + neginclp
    lane = _lane_iota((CHUNK, LANES))
    low = lane < CHUNK
    rowmax = jnp.zeros((CHUNK, LANES), F32)
    for h in range(HEADS):
        blk = _head_cols(dl, h // 2)
        mx = jnp.max(jnp.where(low if h % 2 == 0 else ~low, blk, -jnp.inf), axis=-1, keepdims=True)
        rowmax = jnp.where(lane == _L_F + h, mx, rowmax)
    qk_ref[...] = qk
    dl_ref[...] = dl
    gate_ref[0] = cs
    gate_ref[1] = cs[CHUNK - 1:CHUNK, :] - cs + pltpu.roll(gates, _L_F - _L_I, 1)
    gate_ref[2] = rowmax


def _mlstm_scan(q, k, v, og, mnorm, c_ref, m_ref, qk_ref, dl_ref, gate_ref, o_ref):
    cs, a, rowmax = gate_ref[0], gate_ref[1], gate_ref[2]
    m_row = m_ref[0:1, :]
    m_inter = cs + m_row
    m_t = jnp.maximum(m_inter, rowmax)
    w_inter = jnp.exp(m_inter - m_t)
    emt = jnp.exp(-m_t)
    s = qk_ref[...] * jnp.exp(dl_ref[...] - _expand(m_t, _L_F))
    qw = q * (ML_DQK ** -0.5) * _expand(w_inter, _L_F)

    b_last = cs[CHUNK - 1:CHUNK, :]
    m_new = jnp.maximum(b_last + m_row, jnp.max(a, axis=0, keepdims=True))
    scale_row = jnp.exp(b_last + m_row - m_new)
    kw = k * _expand(jnp.exp(a - m_new), _L_F)
    m_ref[...] = jnp.broadcast_to(m_new, m_ref.shape)

    ones_col = (_lane_iota((CHUNK, ML_DV)) == 0).astype(F32)
    pairs = [(2 * p, 2 * p + 1) for p in range(HEADS // 2)]
    acc, upd = [], []
    for p, heads in enumerate(pairs):
        vext = jnp.concatenate(
            [jnp.concatenate([_head_cols(v, h), ones_col], axis=1) for h in heads], axis=0)
        lhs = jnp.concatenate([_pair_rows(qw, p), _pair_rows(s, p)], axis=1)
        acc.append(_mm(lhs, jnp.concatenate([c_ref[p], vext], axis=0)))
        upd.append(_mm_tn(_pair_rows(kw, p), vext))
    yield
    out = []
    for p, heads in enumerate(pairs):
        scale = jnp.concatenate(
            [jnp.broadcast_to(scale_row[:, _L_F + h:_L_F + h + 1], (ML_DQK, 2 * ML_DV)) for h in heads], axis=0)
        c_ref[p] = c_ref[p] * scale + upd[p]
    den = []
    for h in range(HEADS):
        rows = slice(CHUNK * (h % 2), CHUNK * (h % 2 + 1))
        d = jnp.maximum(jnp.abs(acc[h // 2][rows, ML_DV:ML_DV + 1]), emt[:, _L_F + h:_L_F + h + 1])
        den.append(jnp.broadcast_to(d, (CHUNK, ML_DV)))
    yield
    out = [acc[h // 2][CHUNK * (h % 2):CHUNK * (h % 2 + 1), :ML_DV] / den[h] for h in range(HEADS)]
    oss = [jnp.mean(o * o, axis=-1, keepdims=True) for o in out]
    yield
    out = [out[h] * lax.rsqrt(oss[h] + NORM_EPS) * _head_cols(mnorm, h) * _sigmoid(_head_cols(og, h))
           for h in range(HEADS)]
    o_ref[...] = jnp.concatenate(out, axis=1).astype(BF16)


def _mixer_kernel(gq_ref, gk_ref, gv_ref, mqka_ref, sm_ref, gz_ref, mqkb_ref, mv_ref, mo_ref,
                  convw_ref, bias_ref, alog_ref, gnorm_ref, mnorm_ref,
                  negincl_ref, strict_ref, d16_ref, eye_ref, l3_ref, headsum_ref,
                  oa_ref, ob_ref, xbuf_ref, s_ref, c_ref, m_ref,
                  lw_ref, u_ref, attn_ref, kdec_ref, egl_ref, qk_ref, dl_ref, gate_ref):
    @pl.when(pl.program_id(0) == 0)
    def _():
        for ref in (xbuf_ref, s_ref, c_ref, m_ref, lw_ref, u_ref, attn_ref, kdec_ref, egl_ref, qk_ref, dl_ref):
            ref[...] = jnp.zeros_like(ref)
        for bi in range(gate_ref.shape[0]):
            gate_ref[bi, 0] = jnp.zeros((CHUNK, SMALL_W), F32)
            gate_ref[bi, 1] = jnp.full((CHUNK, SMALL_W), -jnp.inf, F32)
            gate_ref[bi, 2] = jnp.zeros((CHUNK, SMALL_W), F32)

    cw = convw_ref[...]
    gnorm = gnorm_ref[...]
    mnorm = mnorm_ref[...]
    ones3 = jnp.ones((CHUNK, 3 * CHUNK), BF16)
    consts = (negincl_ref[...], strict_ref[...], d16_ref[...], eye_ref[...], ones3, headsum_ref[...])
    lane = _lane_iota((CHUNK, SMALL_W))
    decay_lanes = (lane >= _L_DECAY) & (lane < _L_DECAY + HEADS)
    f_lanes = (lane >= _L_F) & (lane < _L_F + HEADS)
    neg_a = -jnp.exp(alog_ref[...])
    hq = HEADS * GDN_DK
    hm = HEADS * ML_DQK
    nb = gq_ref.shape[0]
    streams = []
    for bi in range(nb):
        streams.append(_gdn_scan(gz_ref[bi].astype(F32), gnorm, s_ref.at[bi], lw_ref.at[bi], u_ref.at[bi],
                                 attn_ref.at[bi], kdec_ref.at[bi], egl_ref.at[bi], oa_ref.at[bi]))
        mqk = mqkb_ref[bi].astype(F32)
        streams.append(_mlstm_scan(mqk[:, :hm], mqk[:, hm:], mv_ref[bi].astype(F32), mo_ref[bi].astype(F32),
                                   mnorm, c_ref.at[bi], m_ref.at[bi], qk_ref.at[bi], dl_ref.at[bi],
                                   gate_ref.at[bi], ob_ref.at[bi]))

    def prepare(bi):
        x = jnp.concatenate([gq_ref[bi], gk_ref[bi], gv_ref[bi]], axis=1).astype(F32)
        xbuf_ref[bi, HALO:, :] = x
        y = cw[CONV_WIDTH - 1:CONV_WIDTH, :] * x
        for tap in range(1, CONV_WIDTH):
            y = y + cw[CONV_WIDTH - 1 - tap:CONV_WIDTH - tap, :] * xbuf_ref[bi, HALO - tap:HALO - tap + CHUNK, :]
        xbuf_ref[bi, :HALO, :] = x[CHUNK - HALO:, :]
        qkv = _silu(y)

        t = sm_ref[bi] + bias_ref[...]
        sp = _softplus(t)
        beta_blk = _sigmoid(t)
        cs_in = jnp.where(f_lanes, t - sp, jnp.where(decay_lanes, neg_a * sp, 0.0))
        cs = jnp.dot(l3_ref[...], _split3_rows(cs_in), preferred_element_type=F32)
        yield
        mqk = mqka_ref[bi].astype(F32)
        yield from _round_robin([
            _gdn_prepare(qkv[:, :hq], qkv[:, hq:2 * hq], qkv[:, 2 * hq:], beta_blk, cs, consts,
                         lw_ref.at[bi], u_ref.at[bi], attn_ref.at[bi], kdec_ref.at[bi], egl_ref.at[bi]),
            _mlstm_prepare(mqk[:, :hm], mqk[:, hm:], t, cs, consts,
                           qk_ref.at[bi], dl_ref.at[bi], gate_ref.at[bi])])

    for _ in _round_robin([prepare(bi) for bi in range(nb)] + streams):
        pass


def _mixer_constants():
    i = np.arange(CHUNK)[:, None]
    j = np.tile(np.arange(CHUNK), HEADS)[None, :]
    negincl = np.where(i >= j, 0.0, -np.inf).astype(np.float32)
    strict = (i > j).astype(np.float32)
    d16 = ((i // SUB) == (j // SUB)).astype(np.float32)
    eye = (i == j).astype(np.float32)
    lower = (np.arange(CHUNK)[:, None] >= np.arange(CHUNK)[None, :]).astype(np.float32)
    l3 = np.tile(lower, (1, 3))
    headsum = (np.arange(2 * HEADS * GDN_DK)[:, None] // GDN_DK == np.arange(SMALL_W)[None, :]).astype(np.float32)
    return (jnp.asarray(negincl), jnp.asarray(strict), jnp.asarray(d16), jnp.asarray(eye),
            jnp.asarray(l3, dtype=BF16), jnp.asarray(headsum, dtype=BF16))


def _mixer(proj, small, conv_w, bias_row, alog_row, gnorm, mnorm, consts):
    _, bsz, seq, _ = proj.shape
    n_chunks = seq // CHUNK
    conv_ch = conv_w.shape[1]

    def prep(c):
        return jnp.minimum(c, n_chunks - 1)

    def scan(c):
        return jnp.maximum(c - 1, 0)

    def pblk(col, chunk_of):
        return pl.BlockSpec((None, bsz, CHUNK, COL), lambda c: (col, 0, chunk_of(c), 0))

    def const(a):
        return pl.BlockSpec(a.shape, lambda c: (0,) * a.ndim)

    pair = HEADS // 2
    fixed = (conv_w, bias_row, alog_row, gnorm, mnorm) + tuple(consts)
    return pl.pallas_call(
        _mixer_kernel,
        grid=(n_chunks + 1,),
        in_specs=[pblk(_GQ, prep), pblk(_GK, prep), pblk(_GV, prep), pblk(_MQK, prep),
                  pl.BlockSpec((bsz, CHUNK, SMALL_W), lambda c: (0, prep(c), 0)),
                  pblk(_GZ, scan), pblk(_MQK, scan), pblk(_MV, scan), pblk(_MO, scan)]
                 + [const(a) for a in fixed],
        out_specs=[pl.BlockSpec((bsz, CHUNK, HEADS * GDN_DV), lambda c: (0, scan(c), 0)),
                   pl.BlockSpec((bsz, CHUNK, HEADS * ML_DV), lambda c: (0, scan(c), 0))],
        out_shape=[jax.ShapeDtypeStruct((bsz, seq, HEADS * GDN_DV), BF16),
                   jax.ShapeDtypeStruct((bsz, seq, HEADS * ML_DV), BF16)],
        scratch_shapes=[
            pltpu.VMEM((bsz, HALO + CHUNK, conv_ch), F32),
            pltpu.VMEM((bsz, GDN_DK, HEADS * GDN_DV), F32),
            pltpu.VMEM((bsz, pair, 2 * ML_DQK, 2 * ML_DV), F32),
            pltpu.VMEM((bsz, 8, SMALL_W), F32),
            pltpu.VMEM((bsz, HEADS, 2 * CHUNK, GDN_DK), BF16),
            pltpu.VMEM((bsz, HEADS * CHUNK, GDN_DV), F32),
            pltpu.VMEM((bsz, pair, 2 * CHUNK, LANES), BF16),
            pltpu.VMEM((bsz, pair, 2 * CHUNK, GDN_DK), BF16),
            pltpu.VMEM((bsz, 8, SMALL_W), F32),
            pltpu.VMEM((bsz, CHUNK, PACKED), F32),
            pltpu.VMEM((bsz, CHUNK, PACKED), F32),
            pltpu.VMEM((bsz, 3, CHUNK, SMALL_W), F32),
        ],
        compiler_params=pltpu.CompilerParams(
            dimension_semantics=("arbitrary",), vmem_limit_bytes=VMEM_LIMIT),
        name="mixer",
    )(proj, proj, proj, proj, small, proj, proj, proj, proj, *fixed)


def _tail_kernel(oa_ref, ob_ref, ga0_ref, ga1_ref, gb0_ref, gb1_ref, x_ref, p_ref,
                 wa_ref, wb_ref, wo_ref, w1_ref, w3_ref, w2_ref, wpg_ref, wple_ref,
                 gffn_ref, gple_ref, gfin_ref, out_ref, *, final, splits):
    ya = jnp.dot(oa_ref[...], wa_ref[...], preferred_element_type=F32)
    yb = jnp.dot(ob_ref[...], wb_ref[...], preferred_element_type=F32)
    ga = jnp.concatenate([ga0_ref[...], ga1_ref[...]], axis=1).astype(F32)
    gb = jnp.concatenate([gb0_ref[...], gb1_ref[...]], axis=1).astype(F32)
    mixed = _sigmoid(ga) * ya + _sigmoid(gb) * yb
    x1 = x_ref[...] + jnp.dot(mixed.astype(BF16), wo_ref[...], preferred_element_type=F32)

    h = _rms(x1, gffn_ref[...]).astype(BF16)
    x2 = x1
    for lo, hi in splits:
        a = jnp.dot(h, w1_ref[:, lo:hi], preferred_element_type=F32)
        b = jnp.dot(h, w3_ref[:, lo:hi], preferred_element_type=F32)
        x2 = x2 + jnp.dot((_silu(a) * b).astype(BF16), w2_ref[lo:hi, :], preferred_element_type=F32)

    gate = _sigmoid(jnp.dot(_rms(x2, gple_ref[...]).astype(BF16), wpg_ref[...], preferred_element_type=F32))
    x3 = x2 + gate * jnp.dot(p_ref[...].astype(BF16), wple_ref[...], preferred_element_type=F32)
    if final:
        x3 = _rms(x3, gfin_ref[...])
    out_ref[...] = x3


def _tail(oa, ob, proj, x, p, weights, gains, tm, final):
    t, d = x.shape
    dff = weights[3].shape[1]
    half = (dff // MXU_WIDTH // 2) * MXU_WIDTH
    splits = ((0, half), (half, dff))

    def rows(width):
        return pl.BlockSpec((tm, width), lambda i: (i, 0))

    def gate(col):
        return pl.BlockSpec((None, tm, COL), lambda i: (col, i, 0))

    def resident(a):
        return pl.BlockSpec(a.shape, lambda i: (0, 0), pipeline_mode=pl.Buffered(1))

    return pl.pallas_call(
        functools.partial(_tail_kernel, final=final, splits=splits),
        grid=(t // tm,),
        in_specs=[rows(oa.shape[1]), rows(ob.shape[1]), gate(_GA), gate(_GA + 1), gate(_GB), gate(_GB + 1),
                  rows(d), rows(p.shape[1])] + [resident(a) for a in weights + gains],
        out_specs=rows(d),
        out_shape=jax.ShapeDtypeStruct((t, d), F32),
        compiler_params=pltpu.CompilerParams(
            dimension_semantics=("parallel",), vmem_limit_bytes=VMEM_LIMIT),
        name="tail_final" if final else "tail",
    )(oa, ob, proj, proj, proj, proj, x, p, *weights, *gains)


def _pack_kernel(w_ref, o_ref, *, runs):
    w = w_ref[...]
    o_ref[...] = jnp.concatenate([w[:, lo:hi] for lo, hi in runs], axis=1).astype(BF16)


def _pack_w_in(w_in):
    depth, d, width = w_in.shape
    g_w, m_w = 4 * HEADS * GDN_DK, 2 * HEADS * ML_DQK + 2 * HEADS * ML_DV
    m0 = g_w + 2 * HEADS
    gates0 = m0 + m_w + 2 * HEADS
    assert gates0 + 2 * d == width
    runs = ((gates0, width), (0, g_w), (m0, m0 + m_w))
    rows = min(256, d)
    main = pl.pallas_call(
        functools.partial(_pack_kernel, runs=runs),
        grid=(depth, d // rows),
        in_specs=[pl.BlockSpec((None, rows, width), lambda l, r: (l, r, 0))],
        out_specs=pl.BlockSpec((None, rows, PACK_W), lambda l, r: (l, r, 0)),
        out_shape=jax.ShapeDtypeStruct((depth, d, PACK_W), BF16),
        compiler_params=pltpu.CompilerParams(
            dimension_semantics=("parallel", "parallel"), vmem_limit_bytes=VMEM_LIMIT),
        name="pack_w_in",
    )(w_in)
    small = jnp.concatenate([w_in[:, :, g_w:m0], w_in[:, :, m0 + m_w:gates0]], axis=-1)
    small = jnp.pad(small, ((0, 0), (0, 0), (0, SMALL_W - 4 * HEADS))).astype(BF16)
    return main, small


def _gate_rows(values, lane0):
    return jnp.pad(values, ((0, 0), (lane0, SMALL_W - lane0 - values.shape[1])))[:, None, :]


def kernel(x, p, g_mix, w_in, conv_w, a_log, dt_bias, gdn_norm, ml_i_bias, ml_f_bias, ml_norm, w_branch_a,
           w_branch_b, w_out, g_ffn, w1, w3, w2, g_ple, w_ple_gate, w_ple, g_final):
    bsz, seq, d = x.shape
    depth = w_in.shape[0]
    t = bsz * seq
    assert w_in.shape[2] == PACK_W + 4 * HEADS and seq % CHUNK == 0

    w_main, w_small = _pack_w_in(w_in)
    wa, wb, wo = w_branch_a.astype(BF16), w_branch_b.astype(BF16), w_out.astype(BF16)
    w1b, w3b, w2b = w1.astype(BF16), w3.astype(BF16), w2.astype(BF16)
    wpg, wple = w_ple_gate.astype(BF16), w_ple.astype(BF16)
    bias_rows = _gate_rows(dt_bias, _L_DECAY) + _gate_rows(ml_i_bias, _L_I) + _gate_rows(ml_f_bias, _L_F)
    alog_rows = _gate_rows(a_log, _L_DECAY)
    gnorm = gdn_norm[:, None, :]
    mnorm = ml_norm.reshape(depth, 1, HEADS * ML_DV)
    consts = _mixer_constants()

    tm_in, tm_tail = min(1024, t), min(512, t)
    xt = x.reshape(t, d)
    for i in range(depth):
        proj, small = _inproj(xt, g_mix[i][None, :], w_main[i], w_small[i], tm_in)
        oa, ob = _mixer(proj.reshape(-1, bsz, seq, COL), small.reshape(bsz, seq, SMALL_W), conv_w[i],
                        bias_rows[i], alog_rows[i], gnorm[i], mnorm[i], consts)
        weights = (wa[i], wb[i], wo[i], w1b[i], w3b[i], w2b[i], wpg[i], wple[i])
        gains = (g_ffn[i][None, :], g_ple[i][None, :], g_final[None, :])
        xt = _tail(oa.reshape(t, -1), ob.reshape(t, -1), proj, xt, p[i].reshape(t, -1), weights, gains,
                   tm_tail, final=(i == depth - 1))
    return xt.reshape(bsz, seq, d)
```

```python
import functools

import jax
import jax.numpy as jnp
import numpy as np
from jax import lax
from jax.experimental import pallas as pl
from jax.experimental.pallas import tpu as pltpu

F32 = jnp.float32
BF16 = jnp.bfloat16

NORM_EPS = 1e-6
CHUNK = 64
HEADS = 4
GDN_DK = 128
GDN_DV = 128
ML_DQK = 64
ML_DV = 128
CONV_WIDTH = 4
LANES = 128
MXU_WIDTH = 256
PACKED = HEADS * CHUNK
SUB = 16
SMALL_W = 128
COL = 512
HALO = 8

_L_BETA, _L_DECAY, _L_I, _L_F = 0, HEADS, 2 * HEADS, 3 * HEADS

_GA, _GB, _GQ, _GK, _GV, _GZ, _MQK, _MV, _MO = 0, 2, 4, 5, 6, 7, 8, 9, 10
PACK_W = 11 * COL

VMEM_LIMIT = 56 * 1024 * 1024


def _mm(a, b):
    return jnp.dot(a.astype(BF16), b.astype(BF16), preferred_element_type=F32)


def _mm_nt(a, b):
    return lax.dot_general(a.astype(BF16), b.astype(BF16), (((1,), (1,)), ((), ())),
                           preferred_element_type=F32)


def _mm_tn(a, b):
    return lax.dot_general(a.astype(BF16), b.astype(BF16), (((0,), (0,)), ((), ())),
                           preferred_element_type=F32)


def _rms(x, gain):
    return x * lax.rsqrt(jnp.mean(x * x, axis=-1, keepdims=True) + NORM_EPS) * gain


def _sigmoid(x):
    return 0.5 * jnp.tanh(0.5 * x) + 0.5


def _silu(x):
    h = 0.5 * x
    return h * jnp.tanh(h) + h


def _softplus(x):
    return jnp.maximum(x, 0.0) + jnp.log(1.0 + jnp.exp(-jnp.abs(x)))


def _split3_rows(x):
    hi = x.astype(BF16)
    r = x - hi.astype(F32)
    mid = r.astype(BF16)
    lo = (r - mid.astype(F32)).astype(BF16)
    return jnp.concatenate([hi, mid, lo], axis=0)


def _inproj_kernel(x_ref, g_ref, w_ref, ws_ref, o_ref, os_ref):
    hb = _rms(x_ref[...], g_ref[...]).astype(BF16)
    os_ref[...] = jnp.dot(hb, ws_ref[...], preferred_element_type=F32)
    for k in range(o_ref.shape[0]):
        o_ref[k] = jnp.dot(hb, w_ref[:, COL * k:COL * (k + 1)], preferred_element_type=F32).astype(BF16)


def _layer_block(stack, layer):
    return pl.BlockSpec((None,) + stack.shape[1:], lambda i: (layer,) + (0,) * (stack.ndim - 1),
                        pipeline_mode=pl.Buffered(1))


def _inproj(x, gain, w_main, w_small, layer, tm):
    t, d = x.shape
    n_col = w_main.shape[2] // COL
    return pl.pallas_call(
        _inproj_kernel,
        grid=(t // tm,),
        in_specs=[
            pl.BlockSpec((tm, d), lambda i: (i, 0)),
            pl.BlockSpec((1, d), lambda i: (0, 0)),
            _layer_block(w_main, layer),
            _layer_block(w_small, layer),
        ],
        out_specs=[
            pl.BlockSpec((n_col, tm, COL), lambda i: (0, i, 0)),
            pl.BlockSpec((tm, SMALL_W), lambda i: (i, 0)),
        ],
        out_shape=[
            jax.ShapeDtypeStruct((n_col, t, COL), BF16),
            jax.ShapeDtypeStruct((t, SMALL_W), F32),
        ],
        compiler_params=pltpu.CompilerParams(
            dimension_semantics=("parallel",), vmem_limit_bytes=VMEM_LIMIT),
        name="inproj",
    )(x, gain, w_main, w_small)


def _lane_iota(shape):
    return lax.broadcasted_iota(jnp.int32, shape, 1)


def _lane_head(shape):
    return lax.shift_right_logical(_lane_iota(shape), int(np.log2(CHUNK)))


def _expand(blk, lane0):
    low = _lane_iota((CHUNK, LANES)) < CHUNK
    cols = [jnp.broadcast_to(blk[:, lane0 + h:lane0 + h + 1], (CHUNK, LANES)) for h in range(HEADS)]
    return jnp.concatenate([jnp.where(low, cols[2 * p], cols[2 * p + 1]) for p in range(HEADS // 2)], axis=1)


def _row_form(colexp, eyep, ones3):
    return jnp.dot(ones3, _split3_rows(colexp * eyep), preferred_element_type=F32)


def _block_diag(b):
    bb = b.astype(BF16)
    head = _lane_head((CHUNK, PACKED))
    return jnp.concatenate([jnp.where(head == h, bb, jnp.zeros_like(bb)) for h in range(HEADS)], axis=0)


def _pprod(a, b):
    return jnp.dot(a.astype(BF16), _block_diag(b), preferred_element_type=F32)


def _pair_rows(a, p):
    blk = a[:, LANES * p:LANES * (p + 1)]
    low = _lane_iota(blk.shape) < CHUNK
    zero = jnp.zeros_like(blk)
    return jnp.concatenate([jnp.where(low, blk, zero), jnp.where(low, zero, blk)], axis=0)


def _head_cols(a, h, width=LANES):
    return a[:, width * h:width * (h + 1)]


def _round_robin(streams):
    streams = list(streams)
    while streams:
        alive = []
        for g in streams:
            try:
                next(g)
                alive.append(g)
            except StopIteration:
                pass
        streams = alive
        if streams:
            yield


def _gdn_prepare(q, k, v, beta_blk, cs, consts, lw_ref, u_ref, attn_ref, kdec_ref, egl_ref):
    neginclp, strictp, d16p, eyep, ones3, headsum = consts
    qh = [_head_cols(q, h) for h in range(HEADS)]
    kh = [_head_cols(k, h) for h in range(HEADS)]
    ssq = jnp.dot(jnp.concatenate([q * q, k * k], axis=1).astype(BF16), headsum, preferred_element_type=F32)
    egc_blk = jnp.exp(cs)
    gl_row = cs[CHUNK - 1:CHUNK, :]
    kdec_blk = jnp.exp(gl_row - cs)
    egl_row = jnp.exp(gl_row)
    gcol = _expand(cs, _L_DECAY)
    grow = _row_form(gcol, eyep, ones3)
    yield

    inv_norm = lax.rsqrt(ssq + NORM_EPS)
    qn, kn, kb, qg, vb, kbe, kdec = [], [], [], [], [], [], []
    for h in range(HEADS):
        qsc = inv_norm[:, h:h + 1] * (GDN_DK ** -0.5)
        ksc = inv_norm[:, HEADS + h:HEADS + h + 1]
        beta = beta_blk[:, _L_BETA + h:_L_BETA + h + 1]
        egc = egc_blk[:, _L_DECAY + h:_L_DECAY + h + 1]
        qn.append(qh[h] * qsc)
        kn.append(kh[h] * ksc)
        kb.append(kh[h] * (ksc * beta))
        qg.append(qh[h] * (qsc * egc))
        vb.append(_head_cols(v, h) * beta)
        kbe.append(kh[h] * (ksc * beta * egc))
        kdec.append(kh[h] * (ksc * kdec_blk[:, _L_DECAY + h:_L_DECAY + h + 1]))

    zero = jnp.zeros((CHUNK, LANES), BF16)
    kexp = jnp.concatenate(
        [jnp.concatenate([kn[h].astype(BF16) if g == h else zero for g in range(HEADS)], axis=1)
         for h in range(HEADS)], axis=0)
    lhs = jnp.concatenate([jnp.concatenate(qn, axis=1), jnp.concatenate(kb, axis=1)], axis=0)
    gram = _mm_nt(lhs, kexp)
    decay = jnp.exp(gcol - grow + neginclp)
    yield

    attn = gram[:CHUNK] * decay
    m = gram[CHUNK:] * decay * strictp

    md = m * d16p
    lo = m - md
    n1 = -md
    n2 = _pprod(n1, n1)
    yield
    dgi = eyep + n1
    n4 = _pprod(n2, n2)
    t = _pprod(n2, dgi)
    yield
    dgi = dgi + t
    n8 = _pprod(n4, n4)
    t = _pprod(n4, dgi)
    yield
    dgi = dgi + t
    t = _pprod(n8, dgi)
    yield
    dgi = dgi + t
    r = _pprod(dgi, lo)
    yield
    r2 = _pprod(r, r)
    yield
    t = _pprod(r2, dgi)
    yield
    w1 = dgi + t
    t = _pprod(r, w1)
    yield
    tinv = w1 - t

    pairs = [(2 * p, 2 * p + 1) for p in range(HEADS // 2)]
    sol = []
    for p, heads in enumerate(pairs):
        rhs = jnp.concatenate([jnp.concatenate([vb[h], kbe[h]], axis=1) for h in heads], axis=0)
        sol.append(_mm(_pair_rows(tinv, p), rhs))
    yield
    for h in range(HEADS):
        rows = slice(CHUNK * (h % 2), CHUNK * (h % 2 + 1))
        lw_ref[h] = jnp.concatenate([qg[h], sol[h // 2][rows, GDN_DV:]], axis=0).astype(BF16)
        u_ref[CHUNK * h:CHUNK * (h + 1), :] = sol[h // 2][rows, :GDN_DV]
    for p, heads in enumerate(pairs):
        attn_ref[p] = _pair_rows(attn, p).astype(BF16)
        kdec_ref[p] = jnp.concatenate([kdec[h] for h in heads], axis=0).astype(BF16)
    egl_ref[...] = jnp.broadcast_to(egl_row, egl_ref.shape)


def _gdn_scan(z, gnorm, s_ref, lw_ref, u_ref, attn_ref, kdec_ref, egl_ref, o_ref):
    pairs = [(2 * p, 2 * p + 1) for p in range(HEADS // 2)]
    egl_row = egl_ref[0:1, :]
    ph = [jnp.dot(lw_ref[h], s_ref[:, GDN_DV * h:GDN_DV * (h + 1)].astype(BF16), preferred_element_type=F32)
          for h in range(HEADS)]
    yield
    vnew = [u_ref[CHUNK * h:CHUNK * (h + 1), :] - ph[h][CHUNK:] for h in range(HEADS)]
    intra, upd = [], []
    for p, heads in enumerate(pairs):
        v0, v1 = vnew[heads[0]], vnew[heads[1]]
        intra.append(_mm(attn_ref[p], jnp.concatenate([v0, v1], axis=0)))
        zv = jnp.zeros_like(v0)
        vbd = jnp.concatenate([jnp.concatenate([v0, zv], axis=1), jnp.concatenate([zv, v1], axis=1)], axis=0)
        upd.append(_mm_tn(kdec_ref[p], vbd))
    yield
    out = []
    for p, heads in enumerate(pairs):
        egl = jnp.concatenate(
            [jnp.broadcast_to(egl_row[:, _L_DECAY + h:_L_DECAY + h + 1], (GDN_DK, GDN_DV)) for h in heads], axis=1)
        cols = slice(2 * GDN_DV * p, 2 * GDN_DV * (p + 1))
        s_ref[:, cols] = s_ref[:, cols] * egl + upd[p]
        for idx, h in enumerate(heads):
            out.append(ph[h][:CHUNK] + intra[p][CHUNK * idx:CHUNK * (idx + 1)])
    oss = [jnp.mean(o * o, axis=-1, keepdims=True) for o in out]
    yield
    out = [out[h] * lax.rsqrt(oss[h] + NORM_EPS) * gnorm * _silu(_head_cols(z, h)) for h in range(HEADS)]
    o_ref[...] = jnp.concatenate(out, axis=1).astype(BF16)


def _mlstm_prepare(q, k, gates, cs, consts, qk_ref, dl_ref, gate_ref):
    neginclp, _, _, eyep, ones3, _ = consts
    head = _lane_head((CHUNK, PACKED))
    kb = k.astype(BF16)
    kexp = jnp.concatenate([jnp.where(head == h, kb, jnp.zeros_like(kb)) for h in range(HEADS)], axis=0)
    qk = _mm_nt(q * (ML_DQK ** -0.5), kexp)
    bcol = _expand(cs, _L_F)
    crow = _row_form(_expand(gates, _L_I) - bcol, eyep, ones3)
    yield

    dl = bcol + crow + neginclp
    lane = _lane_iota((CHUNK, LANES))
    low = lane < CHUNK
    rowmax = jnp.zeros((CHUNK, LANES), F32)
    for h in range(HEADS):
        blk = _head_cols(dl, h // 2)
        mx = jnp.max(jnp.where(low if h % 2 == 0 else ~low, blk, -jnp.inf), axis=-1, keepdims=True)
        rowmax = jnp.where(lane == _L_F + h, mx, rowmax)
    qk_ref[...] = qk
    dl_ref[...] = dl
    gate_ref[0] = cs
    gate_ref[1] = cs[CHUNK - 1:CHUNK, :] - cs + pltpu.roll(gates, _L_F - _L_I, 1)
    gate_ref[2] = rowmax


def _mlstm_scan(q, k, v, og, mnorm, c_ref, m_ref, qk_ref, dl_ref, gate_ref, o_ref):
    cs, a, rowmax = gate_ref[0], gate_ref[1], gate_ref[2]
    m_row = m_ref[0:1, :]
    m_inter = cs + m_row
    m_t = jnp.maximum(m_inter, rowmax)
    w_inter = jnp.exp(m_inter - m_t)
    emt = jnp.exp(-m_t)
    s = qk_ref[...] * jnp.exp(dl_ref[...] - _expand(m_t, _L_F))
    qw = q * (ML_DQK ** -0.5) * _expand(w_inter, _L_F)

    b_last = cs[CHUNK - 1:CHUNK, :]
    m_new = jnp.maximum(b_last + m_row, jnp.max(a, axis=0, keepdims=True))
    scale_row = jnp.exp(b_last + m_row - m_new)
    kw = k * _expand(jnp.exp(a - m_new), _L_F)
    m_ref[...] = jnp.broadcast_to(m_new, m_ref.shape)

    ones_col = (_lane_iota((CHUNK, ML_DV)) == 0).astype(F32)
    pairs = [(2 * p, 2 * p + 1) for p in range(HEADS // 2)]
    acc, upd = [], []
    for p, heads in enumerate(pairs):
        vext = jnp.concatenate(
            [jnp.concatenate([_head_cols(v, h), ones_col], axis=1) for h in heads], axis=0)
        lhs = jnp.concatenate([_pair_rows(qw, p), _pair_rows(s, p)], axis=1)
        acc.append(_mm(lhs, jnp.concatenate([c_ref[p], vext], axis=0)))
        upd.append(_mm_tn(_pair_rows(kw, p), vext))
    yield
    out = []
    for p, heads in enumerate(pairs):
        scale = jnp.concatenate(
            [jnp.broadcast_to(scale_row[:, _L_F + h:_L_F + h + 1], (ML_DQK, 2 * ML_DV)) for h in heads], axis=0)
        c_ref[p] = c_ref[p] * scale + upd[p]
    den = []
    for h in range(HEADS):
        rows = slice(CHUNK * (h % 2), CHUNK * (h % 2 + 1))
        d = jnp.maximum(jnp.abs(acc[h // 2][rows, ML_DV:ML_DV + 1]), emt[:, _L_F + h:_L_F + h + 1])
        den.append(jnp.broadcast_to(d, (CHUNK, ML_DV)))
    yield
    out = [acc[h // 2][CHUNK * (h % 2):CHUNK * (h % 2 + 1), :ML_DV] / den[h] for h in range(HEADS)]
    oss = [jnp.mean(o * o, axis=-1, keepdims=True) for o in out]
    yield
    out = [out[h] * lax.rsqrt(oss[h] + NORM_EPS) * _head_cols(mnorm, h) * _sigmoid(_head_cols(og, h))
           for h in range(HEADS)]
    o_ref[...] = jnp.concatenate(out, axis=1).astype(BF16)


def _mixer_kernel(gq_ref, gk_ref, gv_ref, mqka_ref, sm_ref, gz_ref, mqkb_ref, mv_ref, mo_ref,
                  convw_ref, bias_ref, alog_ref, gnorm_ref, mnorm_ref,
                  negincl_ref, strict_ref, d16_ref, eye_ref, l3_ref, headsum_ref,
                  oa_ref, ob_ref, xbuf_ref, s_ref, c_ref, m_ref,
                  lw_ref, u_ref, attn_ref, kdec_ref, egl_ref, qk_ref, dl_ref, gate_ref):
    @pl.when(pl.program_id(0) == 0)
    def _():
        for ref in (xbuf_ref, s_ref, c_ref, m_ref, lw_ref, u_ref, attn_ref, kdec_ref, egl_ref, qk_ref, dl_ref):
            ref[...] = jnp.zeros_like(ref)
        for bi in range(gate_ref.shape[0]):
            gate_ref[bi, 0] = jnp.zeros((CHUNK, SMALL_W), F32)
            gate_ref[bi, 1] = jnp.full((CHUNK, SMALL_W), -jnp.inf, F32)
            gate_ref[bi, 2] = jnp.zeros((CHUNK, SMALL_W), F32)

    cw = convw_ref[...]
    gnorm = gnorm_ref[...]
    mnorm = mnorm_ref[...]
    ones3 = jnp.ones((CHUNK, 3 * CHUNK), BF16)
    consts = (negincl_ref[...], strict_ref[...], d16_ref[...], eye_ref[...], ones3, headsum_ref[...])
    lane = _lane_iota((CHUNK, SMALL_W))
    decay_lanes = (lane >= _L_DECAY) & (lane < _L_DECAY + HEADS)
    f_lanes = (lane >= _L_F) & (lane < _L_F + HEADS)
    neg_a = -jnp.exp(alog_ref[...])
    hq = HEADS * GDN_DK
    hm = HEADS * ML_DQK
    nb = gq_ref.shape[0]
    streams = []
    for bi in range(nb):
        streams.append(_gdn_scan(gz_ref[bi].astype(F32), gnorm, s_ref.at[bi], lw_ref.at[bi], u_ref.at[bi],
                                 attn_ref.at[bi], kdec_ref.at[bi], egl_ref.at[bi], oa_ref.at[bi]))
        mqk = mqkb_ref[bi].astype(F32)
        streams.append(_mlstm_scan(mqk[:, :hm], mqk[:, hm:], mv_ref[bi].astype(F32), mo_ref[bi].astype(F32),
                                   mnorm, c_ref.at[bi], m_ref.at[bi], qk_ref.at[bi], dl_ref.at[bi],
                                   gate_ref.at[bi], ob_ref.at[bi]))

    def prepare(bi):
        x = jnp.concatenate([gq_ref[bi], gk_ref[bi], gv_ref[bi]], axis=1).astype(F32)
        xbuf_ref[bi, HALO:, :] = x
        y = cw[CONV_WIDTH - 1:CONV_WIDTH, :] * x
        for tap in range(1, CONV_WIDTH):
            y = y + cw[CONV_WIDTH - 1 - tap:CONV_WIDTH - tap, :] * xbuf_ref[bi, HALO - tap:HALO - tap + CHUNK, :]
        xbuf_ref[bi, :HALO, :] = x[CHUNK - HALO:, :]
        qkv = _silu(y)

        t = sm_ref[bi] + bias_ref[...]
        sp = _softplus(t)
        beta_blk = _sigmoid(t)
        cs_in = jnp.where(f_lanes, t - sp, jnp.where(decay_lanes, neg_a * sp, 0.0))
        cs = jnp.dot(l3_ref[...], _split3_rows(cs_in), preferred_element_type=F32)
        yield
        mqk = mqka_ref[bi].astype(F32)
        yield from _round_robin([
            _gdn_prepare(qkv[:, :hq], qkv[:, hq:2 * hq], qkv[:, 2 * hq:], beta_blk, cs, consts,
                         lw_ref.at[bi], u_ref.at[bi], attn_ref.at[bi], kdec_ref.at[bi], egl_ref.at[bi]),
            _mlstm_prepare(mqk[:, :hm], mqk[:, hm:], t, cs, consts,
                           qk_ref.at[bi], dl_ref.at[bi], gate_ref.at[bi])])

    for _ in _round_robin([prepare(bi) for bi in range(nb)] + streams):
        pass


def _mixer_constants():
    i = np.arange(CHUNK)[:, None]
    j = np.tile(np.arange(CHUNK), HEADS)[None, :]
    negincl = np.where(i >= j, 0.0, -np.inf).astype(np.float32)
    strict = (i > j).astype(np.float32)
    d16 = ((i // SUB) == (j // SUB)).astype(np.float32)
    eye = (i == j).astype(np.float32)
    lower = (np.arange(CHUNK)[:, None] >= np.arange(CHUNK)[None, :]).astype(np.float32)
    l3 = np.tile(lower, (1, 3))
    headsum = (np.arange(2 * HEADS * GDN_DK)[:, None] // GDN_DK == np.arange(SMALL_W)[None, :]).astype(np.float32)
    return (jnp.asarray(negincl), jnp.asarray(strict), jnp.asarray(d16), jnp.asarray(eye),
            jnp.asarray(l3, dtype=BF16), jnp.asarray(headsum, dtype=BF16))


def _mixer(proj, small, conv_w, bias_row, alog_row, gnorm, mnorm, consts):
    _, bsz, seq, _ = proj.shape
    n_chunks = seq // CHUNK
    conv_ch = conv_w.shape[1]

    def prep(c):
        return jnp.minimum(c, n_chunks - 1)

    def scan(c):
        return jnp.maximum(c - 1, 0)

    def pblk(col, chunk_of):
        return pl.BlockSpec((None, bsz, CHUNK, COL), lambda c: (col, 0, chunk_of(c), 0))

    def const(a):
        return pl.BlockSpec(a.shape, lambda c: (0,) * a.ndim)

    pair = HEADS // 2
    fixed = (conv_w, bias_row, alog_row, gnorm, mnorm) + tuple(consts)
    return pl.pallas_call(
        _mixer_kernel,
        grid=(n_chunks + 1,),
        in_specs=[pblk(_GQ, prep), pblk(_GK, prep), pblk(_GV, prep), pblk(_MQK, prep),
                  pl.BlockSpec((bsz, CHUNK, SMALL_W), lambda c: (0, prep(c), 0)),
                  pblk(_GZ, scan), pblk(_MQK, scan), pblk(_MV, scan), pblk(_MO, scan)]
                 + [const(a) for a in fixed],
        out_specs=[pl.BlockSpec((bsz, CHUNK, HEADS * GDN_DV), lambda c: (0, scan(c), 0)),
                   pl.BlockSpec((bsz, CHUNK, HEADS * ML_DV), lambda c: (0, scan(c), 0))],
        out_shape=[jax.ShapeDtypeStruct((bsz, seq, HEADS * GDN_DV), BF16),
                   jax.ShapeDtypeStruct((bsz, seq, HEADS * ML_DV), BF16)],
        scratch_shapes=[
            pltpu.VMEM((bsz, HALO + CHUNK, conv_ch), F32),
            pltpu.VMEM((bsz, GDN_DK, HEADS * GDN_DV), F32),
            pltpu.VMEM((bsz, pair, 2 * ML_DQK, 2 * ML_DV), F32),
            pltpu.VMEM((bsz, 8, SMALL_W), F32),
            pltpu.VMEM((bsz, HEADS, 2 * CHUNK, GDN_DK), BF16),
            pltpu.VMEM((bsz, HEADS * CHUNK, GDN_DV), F32),
            pltpu.VMEM((bsz, pair, 2 * CHUNK, LANES), BF16),
            pltpu.VMEM((bsz, pair, 2 * CHUNK, GDN_DK), BF16),
            pltpu.VMEM((bsz, 8, SMALL_W), F32),
            pltpu.VMEM((bsz, CHUNK, PACKED), F32),
            pltpu.VMEM((bsz, CHUNK, PACKED), F32),
            pltpu.VMEM((bsz, 3, CHUNK, SMALL_W), F32),
        ],
        compiler_params=pltpu.CompilerParams(
            dimension_semantics=("arbitrary",), vmem_limit_bytes=VMEM_LIMIT),
        name="mixer",
    )(proj, proj, proj, proj, small, proj, proj, proj, proj, *fixed)


def _tail_kernel(oa_ref, ob_ref, ga0_ref, ga1_ref, gb0_ref, gb1_ref, x_ref, p_ref,
                 wa_ref, wb_ref, wo_ref, w1_ref, w3_ref, w2_ref, wpg_ref, wple_ref,
                 gffn_ref, gple_ref, gfin_ref, out_ref, *, final, splits):
    ya = jnp.dot(oa_ref[...], wa_ref[...], preferred_element_type=F32)
    yb = jnp.dot(ob_ref[...], wb_ref[...], preferred_element_type=F32)
    ga = jnp.concatenate([ga0_ref[...], ga1_ref[...]], axis=1).astype(F32)
    gb = jnp.concatenate([gb0_ref[...], gb1_ref[...]], axis=1).astype(F32)
    mixed = _sigmoid(ga) * ya + _sigmoid(gb) * yb
    x1 = x_ref[...] + jnp.dot(mixed.astype(BF16), wo_ref[...], preferred_element_type=F32)

    h = _rms(x1, gffn_ref[...]).astype(BF16)
    x2 = x1
    for lo, hi in splits:
        a = jnp.dot(h, w1_ref[:, lo:hi], preferred_element_type=F32)
        b = jnp.dot(h, w3_ref[:, lo:hi], preferred_element_type=F32)
        x2 = x2 + jnp.dot((_silu(a) * b).astype(BF16), w2_ref[lo:hi, :], preferred_element_type=F32)

    gate = _sigmoid(jnp.dot(_rms(x2, gple_ref[...]).astype(BF16), wpg_ref[...], preferred_element_type=F32))
    x3 = x2 + gate * jnp.dot(p_ref[...].astype(BF16), wple_ref[...], preferred_element_type=F32)
    if final:
        x3 = _rms(x3, gfin_ref[...])
    out_ref[...] = x3


def _tail(oa, ob, proj, x, p, weights, gains, layer, tm, final):
    t, d = x.shape
    dff = weights[3].shape[2]
    half = (dff // MXU_WIDTH // 2) * MXU_WIDTH
    splits = ((0, half), (half, dff))
    steps = t // tm

    def rows(width):
        return pl.BlockSpec((tm, width), lambda i: (i, 0))

    def gate(col):
        return pl.BlockSpec((None, tm, COL), lambda i: (col, i, 0))

    def resident(a):
        return pl.BlockSpec(a.shape, lambda i: (0, 0), pipeline_mode=pl.Buffered(1))

    return pl.pallas_call(
        functools.partial(_tail_kernel, final=final, splits=splits),
        grid=(steps,),
        in_specs=[rows(oa.shape[1]), rows(ob.shape[1]), gate(_GA), gate(_GA + 1), gate(_GB), gate(_GB + 1),
                  rows(d), pl.BlockSpec((tm, p.shape[1]), lambda i: (layer * steps + i, 0))]
                 + [_layer_block(a, layer) for a in weights] + [resident(a) for a in gains],
        out_specs=rows(d),
        out_shape=jax.ShapeDtypeStruct((t, d), F32),
        compiler_params=pltpu.CompilerParams(
            dimension_semantics=("parallel",), vmem_limit_bytes=VMEM_LIMIT),
        name="tail_final" if final else "tail",
    )(oa, ob, proj, proj, proj, proj, x, p, *weights, *gains)


def _pack_kernel(w_ref, o_ref, os_ref, *, runs, small_windows):
    w = w_ref[...]
    o_ref[...] = jnp.concatenate([w[:, lo:hi] for lo, hi in runs], axis=1).astype(BF16)
    (a0, a_lanes), (b0, b_lanes) = small_windows
    lane = _lane_iota((w.shape[0], SMALL_W))
    small = jnp.where(lane < a_lanes, w[:, a0:a0 + SMALL_W], jnp.where(lane < b_lanes, w[:, b0:b0 + SMALL_W], 0.0))
    os_ref[...] = small.astype(BF16)


def _pack_w_in(w_in):
    depth, d, width = w_in.shape
    g_w, m_w = 4 * HEADS * GDN_DK, 2 * HEADS * ML_DQK + 2 * HEADS * ML_DV
    m0 = g_w + 2 * HEADS
    gates0 = m0 + m_w + 2 * HEADS
    assert gates0 + 2 * d == width
    runs = ((gates0, width), (0, g_w), (m0, m0 + m_w))
    i0 = m0 + m_w
    assert g_w % LANES == 0 and i0 % LANES == _L_I
    small_windows = ((g_w, _L_I), (i0 - _L_I, _L_I + 2 * HEADS))
    rows = min(256, d)
    return pl.pallas_call(
        functools.partial(_pack_kernel, runs=runs, small_windows=small_windows),
        grid=(depth, d // rows),
        in_specs=[pl.BlockSpec((None, rows, width), lambda l, r: (l, r, 0))],
        out_specs=[pl.BlockSpec((None, rows, PACK_W), lambda l, r: (l, r, 0)),
                   pl.BlockSpec((None, rows, SMALL_W), lambda l, r: (l, r, 0))],
        out_shape=[jax.ShapeDtypeStruct((depth, d, PACK_W), BF16),
                   jax.ShapeDtypeStruct((depth, d, SMALL_W), BF16)],
        compiler_params=pltpu.CompilerParams(
            dimension_semantics=("parallel", "parallel"), vmem_limit_bytes=VMEM_LIMIT),
        name="pack_w_in",
    )(w_in)


def _gate_rows(values, lane0):
    return jnp.pad(values, ((0, 0), (lane0, SMALL_W - lane0 - values.shape[1])))[:, None, :]


def kernel(x, p, g_mix, w_in, conv_w, a_log, dt_bias, gdn_norm, ml_i_bias, ml_f_bias, ml_norm, w_branch_a,
           w_branch_b, w_out, g_ffn, w1, w3, w2, g_ple, w_ple_gate, w_ple, g_final):
    bsz, seq, d = x.shape
    depth = w_in.shape[0]
    t = bsz * seq
    assert w_in.shape[2] == PACK_W + 4 * HEADS and seq % CHUNK == 0

    w_main, w_small = _pack_w_in(w_in)
    wa, wb, wo = w_branch_a.astype(BF16), w_branch_b.astype(BF16), w_out.astype(BF16)
    w1b, w3b, w2b = w1.astype(BF16), w3.astype(BF16), w2.astype(BF16)
    wpg, wple = w_ple_gate.astype(BF16), w_ple.astype(BF16)
    bias_rows = _gate_rows(dt_bias, _L_DECAY) + _gate_rows(ml_i_bias, _L_I) + _gate_rows(ml_f_bias, _L_F)
    alog_rows = _gate_rows(a_log, _L_DECAY)
    gnorm = gdn_norm[:, None, :]
    mnorm = ml_norm.reshape(depth, 1, HEADS * ML_DV)
    consts = _mixer_constants()

    tm_in, tm_tail = min(1024, t), min(512, t)
    xt = x.reshape(t, d)
    p_rows = p.reshape(depth * t, p.shape[-1])
    weights = (wa, wb, wo, w1b, w3b, w2b, wpg, wple)
    for i in range(depth):
        proj, small = _inproj(xt, g_mix[i][None, :], w_main, w_small, i, tm_in)
        oa, ob = _mixer(proj.reshape(-1, bsz, seq, COL), small.reshape(bsz, seq, SMALL_W), conv_w[i],
                        bias_rows[i], alog_rows[i], gnorm[i], mnorm[i], consts)
        gains = (g_ffn[i][None, :], g_ple[i][None, :], g_final[None, :])
        xt = _tail(oa.reshape(t, -1), ob.reshape(t, -1), proj, xt, p_rows, weights, gains, i,
                   tm_tail, final=(i == depth - 1))
    return xt.reshape(bsz, seq, d)
```

```python
import functools

import jax
import jax.numpy as jnp
import numpy as np
from jax import lax
from jax.experimental import pallas as pl
from jax.experimental.pallas import tpu as pltpu

F32 = jnp.float32
BF16 = jnp.bfloat16

NORM_EPS = 1e-6
CHUNK = 64
HEADS = 4
GDN_DK = 128
GDN_DV = 128
ML_DQK = 64
ML_DV = 128
CONV_WIDTH = 4
LANES = 128
MXU_WIDTH = 256
PACKED = HEADS * CHUNK
SUB = 16
SMALL_W = 128
COL = 512
HALO = 8

_L_BETA, _L_DECAY, _L_I, _L_F = 0, HEADS, 2 * HEADS, 3 * HEADS

_GA, _GB, _GQ, _GK, _GV, _GZ, _MQK, _MV, _MO = 0, 2, 4, 5, 6, 7, 8, 9, 10
PACK_W = 11 * COL

VMEM_LIMIT = 56 * 1024 * 1024


def _mm(a, b):
    return jnp.dot(a.astype(BF16), b.astype(BF16), preferred_element_type=F32)


def _mm_nt(a, b):
    return lax.dot_general(a.astype(BF16), b.astype(BF16), (((1,), (1,)), ((), ())),
                           preferred_element_type=F32)


def _mm_tn(a, b):
    return lax.dot_general(a.astype(BF16), b.astype(BF16), (((0,), (0,)), ((), ())),
                           preferred_element_type=F32)


def _rms(x, gain):
    return x * lax.rsqrt(jnp.mean(x * x, axis=-1, keepdims=True) + NORM_EPS) * gain


def _sigmoid(x):
    return 0.5 * jnp.tanh(0.5 * x) + 0.5


def _silu(x):
    h = 0.5 * x
    return h * jnp.tanh(h) + h


def _softplus(x):
    return jnp.maximum(x, 0.0) + jnp.log(1.0 + jnp.exp(-jnp.abs(x)))


def _split3_rows(x):
    hi = x.astype(BF16)
    r = x - hi.astype(F32)
    mid = r.astype(BF16)
    lo = (r - mid.astype(F32)).astype(BF16)
    return jnp.concatenate([hi, mid, lo], axis=0)


def _inproj_kernel(x_ref, g_ref, w_ref, ws_ref, o_ref, os_ref):
    hb = _rms(x_ref[...], g_ref[...]).astype(BF16)
    os_ref[...] = jnp.dot(hb, ws_ref[...], preferred_element_type=F32)
    for k in range(o_ref.shape[0]):
        o_ref[k] = jnp.dot(hb, w_ref[:, COL * k:COL * (k + 1)], preferred_element_type=F32).astype(BF16)


def _layer_block(stack, layer):
    return pl.BlockSpec((None,) + stack.shape[1:], lambda i: (layer,) + (0,) * (stack.ndim - 1),
                        pipeline_mode=pl.Buffered(1))


def _inproj(x, gain, w_main, w_small, layer, tm):
    t, d = x.shape
    n_col = w_main.shape[2] // COL
    return pl.pallas_call(
        _inproj_kernel,
        grid=(t // tm,),
        in_specs=[
            pl.BlockSpec((tm, d), lambda i: (i, 0)),
            pl.BlockSpec((1, d), lambda i: (0, 0)),
            _layer_block(w_main, layer),
            _layer_block(w_small, layer),
        ],
        out_specs=[
            pl.BlockSpec((n_col, tm, COL), lambda i: (0, i, 0)),
            pl.BlockSpec((tm, SMALL_W), lambda i: (i, 0)),
        ],
        out_shape=[
            jax.ShapeDtypeStruct((n_col, t, COL), BF16),
            jax.ShapeDtypeStruct((t, SMALL_W), F32),
        ],
        compiler_params=pltpu.CompilerParams(
            dimension_semantics=("parallel",), vmem_limit_bytes=VMEM_LIMIT),
        name="inproj",
    )(x, gain, w_main, w_small)


def _lane_iota(shape):
    return lax.broadcasted_iota(jnp.int32, shape, 1)


def _lane_head(shape):
    return lax.shift_right_logical(_lane_iota(shape), int(np.log2(CHUNK)))


def _expand(blk, lane0):
    low = _lane_iota((CHUNK, LANES)) < CHUNK
    cols = [jnp.broadcast_to(blk[:, lane0 + h:lane0 + h + 1], (CHUNK, LANES)) for h in range(HEADS)]
    return jnp.concatenate([jnp.where(low, cols[2 * p], cols[2 * p + 1]) for p in range(HEADS // 2)], axis=1)


def _row_form(colexp, eyep, ones3):
    return jnp.dot(ones3, _split3_rows(colexp * eyep), preferred_element_type=F32)


def _block_diag(b):
    bb = b.astype(BF16)
    head = _lane_head((CHUNK, PACKED))
    return jnp.concatenate([jnp.where(head == h, bb, jnp.zeros_like(bb)) for h in range(HEADS)], axis=0)


def _pprod(a, b):
    return jnp.dot(a.astype(BF16), _block_diag(b), preferred_element_type=F32)


def _pair_rows(a, p):
    blk = a[:, LANES * p:LANES * (p + 1)]
    low = _lane_iota(blk.shape) < CHUNK
    zero = jnp.zeros_like(blk)
    return jnp.concatenate([jnp.where(low, blk, zero), jnp.where(low, zero, blk)], axis=0)


def _head_cols(a, h, width=LANES):
    return a[:, width * h:width * (h + 1)]


def _round_robin(streams):
    streams = list(streams)
    while streams:
        alive = []
        for g in streams:
            try:
                next(g)
                alive.append(g)
            except StopIteration:
                pass
        streams = alive
        if streams:
            yield


def _gdn_prepare(q, k, v, beta_blk, cs, consts, lw_ref, u_ref, attn_ref, kdec_ref, egl_ref):
    neginclp, strictp, d16p, eyep, ones3, headsum = consts
    qh = [_head_cols(q, h) for h in range(HEADS)]
    kh = [_head_cols(k, h) for h in range(HEADS)]
    ssq = jnp.dot(jnp.concatenate([q * q, k * k], axis=1).astype(BF16), headsum, preferred_element_type=F32)
    egc_blk = jnp.exp(cs)
    gl_row = cs[CHUNK - 1:CHUNK, :]
    kdec_blk = jnp.exp(gl_row - cs)
    egl_row = jnp.exp(gl_row)
    gcol = _expand(cs, _L_DECAY)
    grow = _row_form(gcol, eyep, ones3)
    yield

    inv_norm = lax.rsqrt(ssq + NORM_EPS)
    qn, kn, kb, qg, vb, kbe, kdec = [], [], [], [], [], [], []
    for h in range(HEADS):
        qsc = inv_norm[:, h:h + 1] * (GDN_DK ** -0.5)
        ksc = inv_norm[:, HEADS + h:HEADS + h + 1]
        beta = beta_blk[:, _L_BETA + h:_L_BETA + h + 1]
        egc = egc_blk[:, _L_DECAY + h:_L_DECAY + h + 1]
        qn.append(qh[h] * qsc)
        kn.append(kh[h] * ksc)
        kb.append(kh[h] * (ksc * beta))
        qg.append(qh[h] * (qsc * egc))
        vb.append(_head_cols(v, h) * beta)
        kbe.append(kh[h] * (ksc * beta * egc))
        kdec.append(kh[h] * (ksc * kdec_blk[:, _L_DECAY + h:_L_DECAY + h + 1]))

    zero = jnp.zeros((CHUNK, LANES), BF16)
    kexp = jnp.concatenate(
        [jnp.concatenate([kn[h].astype(BF16) if g == h else zero for g in range(HEADS)], axis=1)
         for h in range(HEADS)], axis=0)
    lhs = jnp.concatenate([jnp.concatenate(qn, axis=1), jnp.concatenate(kb, axis=1)], axis=0)
    gram = _mm_nt(lhs, kexp)
    decay = jnp.exp(gcol - grow + neginclp)
    yield

    attn = gram[:CHUNK] * decay
    m = gram[CHUNK:] * decay * strictp

    md = m * d16p
    lo = m - md
    n1 = -md
    n2 = _pprod(n1, n1)
    yield
    dgi = eyep + n1
    n4 = _pprod(n2, n2)
    t = _pprod(n2, dgi)
    yield
    dgi = dgi + t
    n8 = _pprod(n4, n4)
    t = _pprod(n4, dgi)
    yield
    dgi = dgi + t
    t = _pprod(n8, dgi)
    yield
    dgi = dgi + t
    r = _pprod(dgi, lo)
    yield
    r2 = _pprod(r, r)
    yield
    t = _pprod(r2, dgi)
    yield
    w1 = dgi + t
    t = _pprod(r, w1)
    yield
    tinv = w1 - t

    pairs = [(2 * p, 2 * p + 1) for p in range(HEADS // 2)]
    sol = []
    for p, heads in enumerate(pairs):
        rhs = jnp.concatenate([jnp.concatenate([vb[h], kbe[h]], axis=1) for h in heads], axis=0)
        sol.append(_mm(_pair_rows(tinv, p), rhs))
    yield
    for h in range(HEADS):
        rows = slice(CHUNK * (h % 2), CHUNK * (h % 2 + 1))
        lw_ref[h] = jnp.concatenate([qg[h], sol[h // 2][rows, GDN_DV:]], axis=0).astype(BF16)
        u_ref[CHUNK * h:CHUNK * (h + 1), :] = sol[h // 2][rows, :GDN_DV]
    for p, heads in enumerate(pairs):
        attn_ref[p] = _pair_rows(attn, p).astype(BF16)
        kdec_ref[p] = jnp.concatenate([kdec[h] for h in heads], axis=0).astype(BF16)
    egl_ref[...] = jnp.broadcast_to(egl_row, egl_ref.shape)


def _gdn_scan(z, gnorm, s_ref, lw_ref, u_ref, attn_ref, kdec_ref, egl_ref, o_ref):
    pairs = [(2 * p, 2 * p + 1) for p in range(HEADS // 2)]
    egl_row = egl_ref[0:1, :]
    ph = [jnp.dot(lw_ref[h], s_ref[:, GDN_DV * h:GDN_DV * (h + 1)].astype(BF16), preferred_element_type=F32)
          for h in range(HEADS)]
    yield
    vnew = [u_ref[CHUNK * h:CHUNK * (h + 1), :] - ph[h][CHUNK:] for h in range(HEADS)]
    intra, upd = [], []
    for p, heads in enumerate(pairs):
        v0, v1 = vnew[heads[0]], vnew[heads[1]]
        intra.append(_mm(attn_ref[p], jnp.concatenate([v0, v1], axis=0)))
        zv = jnp.zeros_like(v0)
        vbd = jnp.concatenate([jnp.concatenate([v0, zv], axis=1), jnp.concatenate([zv, v1], axis=1)], axis=0)
        upd.append(_mm_tn(kdec_ref[p], vbd))
    yield
    out = []
    for p, heads in enumerate(pairs):
        egl = jnp.concatenate(
            [jnp.broadcast_to(egl_row[:, _L_DECAY + h:_L_DECAY + h + 1], (GDN_DK, GDN_DV)) for h in heads], axis=1)
        cols = slice(2 * GDN_DV * p, 2 * GDN_DV * (p + 1))
        s_ref[:, cols] = s_ref[:, cols] * egl + upd[p]
        for idx, h in enumerate(heads):
            out.append(ph[h][:CHUNK] + intra[p][CHUNK * idx:CHUNK * (idx + 1)])
    oss = [jnp.mean(o * o, axis=-1, keepdims=True) for o in out]
    yield
    out = [out[h] * lax.rsqrt(oss[h] + NORM_EPS) * gnorm * _silu(_head_cols(z, h)) for h in range(HEADS)]
    o_ref[...] = jnp.concatenate(out, axis=1).astype(BF16)


def _mlstm_prepare(q, k, gates, cs, consts, qk_ref, dl_ref, gate_ref):
    neginclp, _, _, eyep, ones3, _ = consts
    head = _lane_head((CHUNK, PACKED))
    kb = k.astype(BF16)
    kexp = jnp.concatenate([jnp.where(head == h, kb, jnp.zeros_like(kb)) for h in range(HEADS)], axis=0)
    qk = _mm_nt(q * (ML_DQK ** -0.5), kexp)
    bcol = _expand(cs, _L_F)
    crow = _row_form(_expand(gates, _L_I) - bcol, eyep, ones3)
    yield

    dl = bcol + crow + neginclp
    lane = _lane_iota((CHUNK, LANES))
    low = lane < CHUNK
    rowmax = jnp.zeros((CHUNK, LANES), F32)
    for h in range(HEADS):
        blk = _head_cols(dl, h // 2)
        mx = jnp.max(jnp.where(low if h % 2 == 0 else ~low, blk, -jnp.inf), axis=-1, keepdims=True)
        rowmax = jnp.where(lane == _L_F + h, mx, rowmax)
    qk_ref[...] = qk
    dl_ref[...] = dl
    gate_ref[0] = cs
    gate_ref[1] = cs[CHUNK - 1:CHUNK, :] - cs + pltpu.roll(gates, _L_F - _L_I, 1)
    gate_ref[2] = rowmax


def _mlstm_scan(q, k, v, og, mnorm, c_ref, m_ref, qk_ref, dl_ref, gate_ref, o_ref):
    cs, a, rowmax = gate_ref[0], gate_ref[1], gate_ref[2]
    m_row = m_ref[0:1, :]
    m_inter = cs + m_row
    m_t = jnp.maximum(m_inter, rowmax)
    w_inter = jnp.exp(m_inter - m_t)
    emt = jnp.exp(-m_t)
    s = qk_ref[...] * jnp.exp(dl_ref[...] - _expand(m_t, _L_F))
    qw = q * (ML_DQK ** -0.5) * _expand(w_inter, _L_F)

    b_last = cs[CHUNK - 1:CHUNK, :]
    m_new = jnp.maximum(b_last + m_row, jnp.max(a, axis=0, keepdims=True))
    scale_row = jnp.exp(b_last + m_row - m_new)
    kw = k * _expand(jnp.exp(a - m_new), _L_F)
    m_ref[...] = jnp.broadcast_to(m_new, m_ref.shape)

    ones_col = (_lane_iota((CHUNK, ML_DV)) == 0).astype(F32)
    pairs = [(2 * p, 2 * p + 1) for p in range(HEADS // 2)]
    acc, upd = [], []
    for p, heads in enumerate(pairs):
        vext = jnp.concatenate(
            [jnp.concatenate([_head_cols(v, h), ones_col], axis=1) for h in heads], axis=0)
        lhs = jnp.concatenate([_pair_rows(qw, p), _pair_rows(s, p)], axis=1)
        acc.append(_mm(lhs, jnp.concatenate([c_ref[p], vext], axis=0)))
        upd.append(_mm_tn(_pair_rows(kw, p), vext))
    yield
    out = []
    for p, heads in enumerate(pairs):
        scale = jnp.concatenate(
            [jnp.broadcast_to(scale_row[:, _L_F + h:_L_F + h + 1], (ML_DQK, 2 * ML_DV)) for h in heads], axis=0)
        c_ref[p] = c_ref[p] * scale + upd[p]
    den = []
    for h in range(HEADS):
        rows = slice(CHUNK * (h % 2), CHUNK * (h % 2 + 1))
        d = jnp.maximum(jnp.abs(acc[h // 2][rows, ML_DV:ML_DV + 1]), emt[:, _L_F + h:_L_F + h + 1])
        den.append(jnp.broadcast_to(d, (CHUNK, ML_DV)))
    yield
    out = [acc[h // 2][CHUNK * (h % 2):CHUNK * (h % 2 + 1), :ML_DV] / den[h] for h in range(HEADS)]
    oss = [jnp.mean(o * o, axis=-1, keepdims=True) for o in out]
    yield
    out = [out[h] * lax.rsqrt(oss[h] + NORM_EPS) * _head_cols(mnorm, h) * _sigmoid(_head_cols(og, h))
           for h in range(HEADS)]
    o_ref[...] = jnp.concatenate(out, axis=1).astype(BF16)


def _mixer_kernel(gq_ref, gk_ref, gv_ref, mqka_ref, sm_ref, gz_ref, mqkb_ref, mv_ref, mo_ref,
                  convw_ref, bias_ref, alog_ref, gnorm_ref, mnorm_ref,
                  negincl_ref, strict_ref, d16_ref, eye_ref, l3_ref, headsum_ref,
                  oa_ref, ob_ref, xbuf_ref, s_ref, c_ref, m_ref,
                  lw_ref, u_ref, attn_ref, kdec_ref, egl_ref, qk_ref, dl_ref, gate_ref):
    @pl.when(pl.program_id(0) == 0)
    def _():
        for ref in (xbuf_ref, s_ref, c_ref, m_ref, lw_ref, u_ref, attn_ref, kdec_ref, egl_ref, qk_ref, dl_ref):
            ref[...] = jnp.zeros_like(ref)
        for bi in range(gate_ref.shape[0]):
            gate_ref[bi, 0] = jnp.zeros((CHUNK, SMALL_W), F32)
            gate_ref[bi, 1] = jnp.full((CHUNK, SMALL_W), -jnp.inf, F32)
            gate_ref[bi, 2] = jnp.zeros((CHUNK, SMALL_W), F32)

    cw = convw_ref[...]
    gnorm = gnorm_ref[...]
    mnorm = mnorm_ref[...]
    ones3 = jnp.ones((CHUNK, 3 * CHUNK), BF16)
    consts = (negincl_ref[...], strict_ref[...], d16_ref[...], eye_ref[...], ones3, headsum_ref[...])
    lane = _lane_iota((CHUNK, SMALL_W))
    decay_lanes = (lane >= _L_DECAY) & (lane < _L_DECAY + HEADS)
    f_lanes = (lane >= _L_F) & (lane < _L_F + HEADS)
    neg_a = -jnp.exp(alog_ref[...])
    hq = HEADS * GDN_DK
    hm = HEADS * ML_DQK
    nb = gq_ref.shape[0]
    streams = []
    for bi in range(nb):
        streams.append(_gdn_scan(gz_ref[bi].astype(F32), gnorm, s_ref.at[bi], lw_ref.at[bi], u_ref.at[bi],
                                 attn_ref.at[bi], kdec_ref.at[bi], egl_ref.at[bi], oa_ref.at[bi]))
        mqk = mqkb_ref[bi].astype(F32)
        streams.append(_mlstm_scan(mqk[:, :hm], mqk[:, hm:], mv_ref[bi].astype(F32), mo_ref[bi].astype(F32),
                                   mnorm, c_ref.at[bi], m_ref.at[bi], qk_ref.at[bi], dl_ref.at[bi],
                                   gate_ref.at[bi], ob_ref.at[bi]))

    def prepare(bi):
        x = jnp.concatenate([gq_ref[bi], gk_ref[bi], gv_ref[bi]], axis=1).astype(F32)
        xbuf_ref[bi, HALO:, :] = x
        y = cw[CONV_WIDTH - 1:CONV_WIDTH, :] * x
        for tap in range(1, CONV_WIDTH):
            y = y + cw[CONV_WIDTH - 1 - tap:CONV_WIDTH - tap, :] * xbuf_ref[bi, HALO - tap:HALO - tap + CHUNK, :]
        xbuf_ref[bi, :HALO, :] = x[CHUNK - HALO:, :]
        qkv = _silu(y)

        t = sm_ref[bi] + bias_ref[...]
        sp = _softplus(t)
        beta_blk = _sigmoid(t)
        cs_in = jnp.where(f_lanes, t - sp, jnp.where(decay_lanes, neg_a * sp, 0.0))
        cs = jnp.dot(l3_ref[...], _split3_rows(cs_in), preferred_element_type=F32)
        yield
        mqk = mqka_ref[bi].astype(F32)
        yield from _round_robin([
            _gdn_prepare(qkv[:, :hq], qkv[:, hq:2 * hq], qkv[:, 2 * hq:], beta_blk, cs, consts,
                         lw_ref.at[bi], u_ref.at[bi], attn_ref.at[bi], kdec_ref.at[bi], egl_ref.at[bi]),
            _mlstm_prepare(mqk[:, :hm], mqk[:, hm:], t, cs, consts,
                           qk_ref.at[bi], dl_ref.at[bi], gate_ref.at[bi])])

    for _ in _round_robin([prepare(bi) for bi in range(nb)] + streams):
        pass


def _mixer_constants():
    i = np.arange(CHUNK)[:, None]
    j = np.tile(np.arange(CHUNK), HEADS)[None, :]
    negincl = np.where(i >= j, 0.0, -np.inf).astype(np.float32)
    strict = (i > j).astype(np.float32)
    d16 = ((i // SUB) == (j // SUB)).astype(np.float32)
    eye = (i == j).astype(np.float32)
    lower = (np.arange(CHUNK)[:, None] >= np.arange(CHUNK)[None, :]).astype(np.float32)
    l3 = np.tile(lower, (1, 3))
    headsum = (np.arange(2 * HEADS * GDN_DK)[:, None] // GDN_DK == np.arange(SMALL_W)[None, :]).astype(np.float32)
    return (jnp.asarray(negincl), jnp.asarray(strict), jnp.asarray(d16), jnp.asarray(eye),
            jnp.asarray(l3, dtype=BF16), jnp.asarray(headsum, dtype=BF16))


def _mixer(proj, small, conv_w, bias_row, alog_row, gnorm, mnorm, consts):
    _, bsz, seq, _ = proj.shape
    n_chunks = seq // CHUNK
    conv_ch = conv_w.shape[1]

    def prep(c):
        return jnp.minimum(c, n_chunks - 1)

    def scan(c):
        return jnp.maximum(c - 1, 0)

    def pblk(col, chunk_of):
        return pl.BlockSpec((None, bsz, CHUNK, COL), lambda c: (col, 0, chunk_of(c), 0))

    def const(a):
        return pl.BlockSpec(a.shape, lambda c: (0,) * a.ndim)

    pair = HEADS // 2
    fixed = (conv_w, bias_row, alog_row, gnorm, mnorm) + tuple(consts)
    return pl.pallas_call(
        _mixer_kernel,
        grid=(n_chunks + 1,),
        in_specs=[pblk(_GQ, prep), pblk(_GK, prep), pblk(_GV, prep), pblk(_MQK, prep),
                  pl.BlockSpec((bsz, CHUNK, SMALL_W), lambda c: (0, prep(c), 0)),
                  pblk(_GZ, scan), pblk(_MQK, scan), pblk(_MV, scan), pblk(_MO, scan)]
                 + [const(a) for a in fixed],
        out_specs=[pl.BlockSpec((bsz, CHUNK, HEADS * GDN_DV), lambda c: (0, scan(c), 0)),
                   pl.BlockSpec((bsz, CHUNK, HEADS * ML_DV), lambda c: (0, scan(c), 0))],
        out_shape=[jax.ShapeDtypeStruct((bsz, seq, HEADS * GDN_DV), BF16),
                   jax.ShapeDtypeStruct((bsz, seq, HEADS * ML_DV), BF16)],
        scratch_shapes=[
            pltpu.VMEM((bsz, HALO + CHUNK, conv_ch), F32),
            pltpu.VMEM((bsz, GDN_DK, HEADS * GDN_DV), F32),
            pltpu.VMEM((bsz, pair, 2 * ML_DQK, 2 * ML_DV), F32),
            pltpu.VMEM((bsz, 8, SMALL_W), F32),
            pltpu.VMEM((bsz, HEADS, 2 * CHUNK, GDN_DK), BF16),
            pltpu.VMEM((bsz, HEADS * CHUNK, GDN_DV), F32),
            pltpu.VMEM((bsz, pair, 2 * CHUNK, LANES), BF16),
            pltpu.VMEM((bsz, pair, 2 * CHUNK, GDN_DK), BF16),
            pltpu.VMEM((bsz, 8, SMALL_W), F32),
            pltpu.VMEM((bsz, CHUNK, PACKED), F32),
            pltpu.VMEM((bsz, CHUNK, PACKED), F32),
            pltpu.VMEM((bsz, 3, CHUNK, SMALL_W), F32),
        ],
        compiler_params=pltpu.CompilerParams(
            dimension_semantics=("arbitrary",), vmem_limit_bytes=VMEM_LIMIT),
        name="mixer",
    )(proj, proj, proj, proj, small, proj, proj, proj, proj, *fixed)


def _tail_kernel(oa_ref, ob_ref, ga0_ref, ga1_ref, gb0_ref, gb1_ref, x_ref, p_ref,
                 wa_ref, wb_ref, wo_ref, w1_ref, w3_ref, w2_ref, wpg_ref, wple_ref,
                 gffn_ref, gple_ref, gfin_ref, out_ref, *, final, splits):
    ya = jnp.dot(oa_ref[...], wa_ref[...], preferred_element_type=F32)
    yb = jnp.dot(ob_ref[...], wb_ref[...], preferred_element_type=F32)
    ga = jnp.concatenate([ga0_ref[...], ga1_ref[...]], axis=1).astype(F32)
    gb = jnp.concatenate([gb0_ref[...], gb1_ref[...]], axis=1).astype(F32)
    mixed = _sigmoid(ga) * ya + _sigmoid(gb) * yb
    x1 = x_ref[...] + jnp.dot(mixed.astype(BF16), wo_ref[...], preferred_element_type=F32)

    h = _rms(x1, gffn_ref[...]).astype(BF16)
    x2 = x1
    for lo, hi in splits:
        a = jnp.dot(h, w1_ref[:, lo:hi], preferred_element_type=F32)
        b = jnp.dot(h, w3_ref[:, lo:hi], preferred_element_type=F32)
        x2 = x2 + jnp.dot((_silu(a) * b).astype(BF16), w2_ref[lo:hi, :], preferred_element_type=F32)

    gate = _sigmoid(jnp.dot(_rms(x2, gple_ref[...]).astype(BF16), wpg_ref[...], preferred_element_type=F32))
    x3 = x2 + gate * jnp.dot(p_ref[...].astype(BF16), wple_ref[...], preferred_element_type=F32)
    if final:
        x3 = _rms(x3, gfin_ref[...])
    out_ref[...] = x3


def _tail(oa, ob, proj, x, p, weights, gains, layer, tm, final):
    t, d = x.shape
    dff = weights[3].shape[2]
    half = (dff // MXU_WIDTH // 2) * MXU_WIDTH
    splits = ((0, half), (half, dff))
    steps = t // tm

    def rows(width):
        return pl.BlockSpec((tm, width), lambda i: (i, 0))

    def gate(col):
        return pl.BlockSpec((None, tm, COL), lambda i: (col, i, 0))

    def resident(a):
        return pl.BlockSpec(a.shape, lambda i: (0, 0), pipeline_mode=pl.Buffered(1))

    return pl.pallas_call(
        functools.partial(_tail_kernel, final=final, splits=splits),
        grid=(steps,),
        in_specs=[rows(oa.shape[1]), rows(ob.shape[1]), gate(_GA), gate(_GA + 1), gate(_GB), gate(_GB + 1),
                  rows(d), pl.BlockSpec((tm, p.shape[1]), lambda i: (layer * steps + i, 0))]
                 + [_layer_block(a, layer) for a in weights] + [resident(a) for a in gains],
        out_specs=rows(d),
        out_shape=jax.ShapeDtypeStruct((t, d), F32),
        compiler_params=pltpu.CompilerParams(
            dimension_semantics=("parallel",), vmem_limit_bytes=VMEM_LIMIT),
        name="tail_final" if final else "tail",
    )(oa, ob, proj, proj, proj, proj, x, p, *weights, *gains)


def _pack_kernel(wt_ref, o_ref):
    o_ref[...] = wt_ref[0].T.astype(BF16)


def _pack_small_kernel(a_ref, b_ref, o_ref, *, a_lanes, b_lanes):
    a, b = a_ref[0].T, b_ref[0].T
    lane = _lane_iota(a.shape)
    o_ref[...] = jnp.where(lane < a_lanes, a, jnp.where(lane < b_lanes, b, 0.0)).astype(BF16)


def _pack_w_in(w_in):
    depth, d, width = w_in.shape
    g_w, m_w = 4 * HEADS * GDN_DK, 2 * HEADS * ML_DQK + 2 * HEADS * ML_DV
    m0 = g_w + 2 * HEADS
    gates0 = m0 + m_w + 2 * HEADS
    assert gates0 + 2 * d == width and g_w == (_GZ + 1 - _GQ) * COL and 2 * d == _GQ * COL
    assert gates0 % 8 == 0 and m0 % 8 == 0 and 2 * HEADS == 8
    wt = jnp.swapaxes(w_in, 1, 2)

    def src_row(k):
        row = jnp.where(k < _GQ, gates0 + COL * k, jnp.where(k < _MQK, COL * (k - _GQ), m0 + COL * (k - _MQK)))
        return pl.multiple_of(row, 2 * HEADS)

    params = pltpu.CompilerParams(dimension_semantics=("parallel", "parallel"), vmem_limit_bytes=VMEM_LIMIT)
    main = pl.pallas_call(
        _pack_kernel,
        grid=(depth, PACK_W // COL),
        in_specs=[pl.BlockSpec((pl.Element(1), pl.Element(COL), pl.Element(d)), lambda l, k: (l, src_row(k), 0))],
        out_specs=pl.BlockSpec((None, d, COL), lambda l, k: (l, 0, k)),
        out_shape=jax.ShapeDtypeStruct((depth, d, PACK_W), BF16),
        compiler_params=params,
        name="pack_w_in",
    )(wt)
    i0 = m0 + m_w
    assert g_w % LANES == 0 and i0 % LANES == _L_I

    def window(row0):
        return pl.BlockSpec((pl.Element(1), pl.Element(SMALL_W), pl.Element(d)), lambda l, _: (l, row0, 0))

    small = pl.pallas_call(
        functools.partial(_pack_small_kernel, a_lanes=_L_I, b_lanes=_L_I + 2 * HEADS),
        grid=(depth, 1),
        in_specs=[window(g_w), window(i0 - _L_I)],
        out_specs=pl.BlockSpec((None, d, SMALL_W), lambda l, _: (l, 0, 0)),
        out_shape=jax.ShapeDtypeStruct((depth, d, SMALL_W), BF16),
        compiler_params=params,
        name="pack_w_small",
    )(wt, wt)
    return main, small


def _gate_rows(values, lane0):
    return jnp.pad(values, ((0, 0), (lane0, SMALL_W - lane0 - values.shape[1])))[:, None, :]


def kernel(x, p, g_mix, w_in, conv_w, a_log, dt_bias, gdn_norm, ml_i_bias, ml_f_bias, ml_norm, w_branch_a,
           w_branch_b, w_out, g_ffn, w1, w3, w2, g_ple, w_ple_gate, w_ple, g_final):
    bsz, seq, d = x.shape
    depth = w_in.shape[0]
    t = bsz * seq
    assert w_in.shape[2] == PACK_W + 4 * HEADS and seq % CHUNK == 0

    w_main, w_small = _pack_w_in(w_in)
    wa, wb, wo = w_branch_a.astype(BF16), w_branch_b.astype(BF16), w_out.astype(BF16)
    w1b, w3b, w2b = w1.astype(BF16), w3.astype(BF16), w2.astype(BF16)
    wpg, wple = w_ple_gate.astype(BF16), w_ple.astype(BF16)
    bias_rows = _gate_rows(dt_bias, _L_DECAY) + _gate_rows(ml_i_bias, _L_I) + _gate_rows(ml_f_bias, _L_F)
    alog_rows = _gate_rows(a_log, _L_DECAY)
    gnorm = gdn_norm[:, None, :]
    mnorm = ml_norm.reshape(depth, 1, HEADS * ML_DV)
    consts = _mixer_constants()

    tm_in, tm_tail = min(1024, t), min(512, t)
    xt = x.reshape(t, d)
    p_rows = p.reshape(depth * t, p.shape[-1])
    weights = (wa, wb, wo, w1b, w3b, w2b, wpg, wple)
    for i in range(depth):
        proj, small = _inproj(xt, g_mix[i][None, :], w_main, w_small, i, tm_in)
        oa, ob = _mixer(proj.reshape(-1, bsz, seq, COL), small.reshape(bsz, seq, SMALL_W), conv_w[i],
                        bias_rows[i], alog_rows[i], gnorm[i], mnorm[i], consts)
        gains = (g_ffn[i][None, :], g_ple[i][None, :], g_final[None, :])
        xt = _tail(oa.reshape(t, -1), ob.reshape(t, -1), proj, xt, p_rows, weights, gains, i,
                   tm_tail, final=(i == depth - 1))
    return xt.reshape(bsz, seq, d)
```

```python
import functools

import jax
import jax.numpy as jnp
import numpy as np
from jax import lax
from jax.experimental import pallas as pl
from jax.experimental.pallas import tpu as pltpu

F32 = jnp.float32
BF16 = jnp.bfloat16

NORM_EPS = 1e-6
CHUNK = 64
HEADS = 4
GDN_DK = 128
GDN_DV = 128
ML_DQK = 64
ML_DV = 128
CONV_WIDTH = 4
LANES = 128
MXU_WIDTH = 256
PACKED = HEADS * CHUNK
SUB = 16
SMALL_W = 128
COL = 512
HALO = 8
MIXER_SUBS = 1

_L_BETA, _L_DECAY, _L_I, _L_F = 0, HEADS, 2 * HEADS, 3 * HEADS

_GA, _GB, _GQ, _GK, _GV, _GZ, _MQK, _MV, _MO = 0, 2, 4, 5, 6, 7, 8, 9, 10
PACK_W = 11 * COL

VMEM_LIMIT = 56 * 1024 * 1024


def _mm(a, b):
    return jnp.dot(a.astype(BF16), b.astype(BF16), preferred_element_type=F32)


def _mm_nt(a, b):
    return lax.dot_general(a.astype(BF16), b.astype(BF16), (((1,), (1,)), ((), ())),
                           preferred_element_type=F32)


def _mm_tn(a, b):
    return lax.dot_general(a.astype(BF16), b.astype(BF16), (((0,), (0,)), ((), ())),
                           preferred_element_type=F32)


def _rms(x, gain):
    return x * lax.rsqrt(jnp.mean(x * x, axis=-1, keepdims=True) + NORM_EPS) * gain


def _sigmoid(x):
    return 0.5 * jnp.tanh(0.5 * x) + 0.5


def _silu(x):
    h = 0.5 * x
    return h * jnp.tanh(h) + h


def _softplus(x):
    return jnp.maximum(x, 0.0) + jnp.log(1.0 + jnp.exp(-jnp.abs(x)))


def _split3_rows(x):
    hi = x.astype(BF16)
    r = x - hi.astype(F32)
    mid = r.astype(BF16)
    lo = (r - mid.astype(F32)).astype(BF16)
    return jnp.concatenate([hi, mid, lo], axis=0)


def _inproj_kernel(x_ref, g_ref, w_ref, ws_ref, o_ref, os_ref):
    hb = _rms(x_ref[...], g_ref[...]).astype(BF16)
    os_ref[...] = jnp.dot(hb, ws_ref[...], preferred_element_type=F32)
    for k in range(o_ref.shape[0]):
        o_ref[k] = jnp.dot(hb, w_ref[:, COL * k:COL * (k + 1)], preferred_element_type=F32).astype(BF16)


def _layer_block(stack, layer):
    return pl.BlockSpec((None,) + stack.shape[1:], lambda i: (layer,) + (0,) * (stack.ndim - 1),
                        pipeline_mode=pl.Buffered(1))


def _inproj(x, gain, w_main, w_small, layer, tm):
    t, d = x.shape
    n_col = w_main.shape[2] // COL
    return pl.pallas_call(
        _inproj_kernel,
        grid=(t // tm,),
        in_specs=[
            pl.BlockSpec((tm, d), lambda i: (i, 0)),
            pl.BlockSpec((1, d), lambda i: (0, 0)),
            _layer_block(w_main, layer),
            _layer_block(w_small, layer),
        ],
        out_specs=[
            pl.BlockSpec((n_col, tm, COL), lambda i: (0, i, 0)),
            pl.BlockSpec((tm, SMALL_W), lambda i: (i, 0)),
        ],
        out_shape=[
            jax.ShapeDtypeStruct((n_col, t, COL), BF16),
            jax.ShapeDtypeStruct((t, SMALL_W), F32),
        ],
        compiler_params=pltpu.CompilerParams(
            dimension_semantics=("parallel",), vmem_limit_bytes=VMEM_LIMIT),
        name="inproj",
    )(x, gain, w_main, w_small)


def _lane_iota(shape):
    return lax.broadcasted_iota(jnp.int32, shape, 1)


def _lane_head(shape):
    return lax.shift_right_logical(_lane_iota(shape), int(np.log2(CHUNK)))


def _expand(blk, lane0):
    low = _lane_iota((CHUNK, LANES)) < CHUNK
    cols = [jnp.broadcast_to(blk[:, lane0 + h:lane0 + h + 1], (CHUNK, LANES)) for h in range(HEADS)]
    return jnp.concatenate([jnp.where(low, cols[2 * p], cols[2 * p + 1]) for p in range(HEADS // 2)], axis=1)


def _row_form(colexp, eyep, ones3):
    return jnp.dot(ones3, _split3_rows(colexp * eyep), preferred_element_type=F32)


def _block_diag(b):
    bb = b.astype(BF16)
    head = _lane_head((CHUNK, PACKED))
    return jnp.concatenate([jnp.where(head == h, bb, jnp.zeros_like(bb)) for h in range(HEADS)], axis=0)


def _pprod(a, b):
    return jnp.dot(a.astype(BF16), _block_diag(b), preferred_element_type=F32)


def _pair_rows(a, p):
    blk = a[:, LANES * p:LANES * (p + 1)]
    low = _lane_iota(blk.shape) < CHUNK
    zero = jnp.zeros_like(blk)
    return jnp.concatenate([jnp.where(low, blk, zero), jnp.where(low, zero, blk)], axis=0)


def _head_cols(a, h, width=LANES):
    return a[:, width * h:width * (h + 1)]


def _round_robin(streams):
    streams = list(streams)
    while streams:
        alive = []
        for g in streams:
            try:
                next(g)
                alive.append(g)
            except StopIteration:
                pass
        streams = alive
        if streams:
            yield


def _gdn_prepare(q, k, v, beta_blk, cs, consts, lw_ref, u_ref, attn_ref, kdec_ref, egl_ref, consumed):
    neginclp, strictp, d16p, eyep, ones3, headsum = consts
    qh = [_head_cols(q, h) for h in range(HEADS)]
    kh = [_head_cols(k, h) for h in range(HEADS)]
    ssq = jnp.dot(jnp.concatenate([q * q, k * k], axis=1).astype(BF16), headsum, preferred_element_type=F32)
    egc_blk = jnp.exp(cs)
    gl_row = cs[CHUNK - 1:CHUNK, :]
    kdec_blk = jnp.exp(gl_row - cs)
    egl_row = jnp.exp(gl_row)
    gcol = _expand(cs, _L_DECAY)
    grow = _row_form(gcol, eyep, ones3)
    yield

    inv_norm = lax.rsqrt(ssq + NORM_EPS)
    qn, kn, kb, qg, vb, kbe, kdec = [], [], [], [], [], [], []
    for h in range(HEADS):
        qsc = inv_norm[:, h:h + 1] * (GDN_DK ** -0.5)
        ksc = inv_norm[:, HEADS + h:HEADS + h + 1]
        beta = beta_blk[:, _L_BETA + h:_L_BETA + h + 1]
        egc = egc_blk[:, _L_DECAY + h:_L_DECAY + h + 1]
        qn.append(qh[h] * qsc)
        kn.append(kh[h] * ksc)
        kb.append(kh[h] * (ksc * beta))
        qg.append(qh[h] * (qsc * egc))
        vb.append(_head_cols(v, h) * beta)
        kbe.append(kh[h] * (ksc * beta * egc))
        kdec.append(kh[h] * (ksc * kdec_blk[:, _L_DECAY + h:_L_DECAY + h + 1]))

    zero = jnp.zeros((CHUNK, LANES), BF16)
    kexp = jnp.concatenate(
        [jnp.concatenate([kn[h].astype(BF16) if g == h else zero for g in range(HEADS)], axis=1)
         for h in range(HEADS)], axis=0)
    lhs = jnp.concatenate([jnp.concatenate(qn, axis=1), jnp.concatenate(kb, axis=1)], axis=0)
    gram = _mm_nt(lhs, kexp)
    decay = jnp.exp(gcol - grow + neginclp)
    yield

    attn = gram[:CHUNK] * decay
    m = gram[CHUNK:] * decay * strictp

    md = m * d16p
    lo = m - md
    n1 = -md
    n2 = _pprod(n1, n1)
    yield
    dgi = eyep + n1
    n4 = _pprod(n2, n2)
    t = _pprod(n2, dgi)
    yield
    dgi = dgi + t
    n8 = _pprod(n4, n4)
    t = _pprod(n4, dgi)
    yield
    dgi = dgi + t
    t = _pprod(n8, dgi)
    yield
    dgi = dgi + t
    r = _pprod(dgi, lo)
    yield
    r2 = _pprod(r, r)
    yield
    t = _pprod(r2, dgi)
    yield
    w1 = dgi + t
    t = _pprod(r, w1)
    yield
    tinv = w1 - t

    pairs = [(2 * p, 2 * p + 1) for p in range(HEADS // 2)]
    sol = []
    for p, heads in enumerate(pairs):
        rhs = jnp.concatenate([jnp.concatenate([vb[h], kbe[h]], axis=1) for h in heads], axis=0)
        sol.append(_mm(_pair_rows(tinv, p), rhs))
    yield
    while not consumed():
        yield
    for h in range(HEADS):
        rows = slice(CHUNK * (h % 2), CHUNK * (h % 2 + 1))
        lw_ref[h] = jnp.concatenate([qg[h], sol[h // 2][rows, GDN_DV:]], axis=0).astype(BF16)
        u_ref[CHUNK * h:CHUNK * (h + 1), :] = sol[h // 2][rows, :GDN_DV]
    for p, heads in enumerate(pairs):
        attn_ref[p] = _pair_rows(attn, p).astype(BF16)
        kdec_ref[p] = jnp.concatenate([kdec[h] for h in heads], axis=0).astype(BF16)
    egl_ref[...] = jnp.broadcast_to(egl_row, egl_ref.shape)


def _gdn_scan(z, gnorm, s_ref, lw_ref, u_ref, attn_ref, kdec_ref, egl_ref, o_ref, mark_consumed):
    pairs = [(2 * p, 2 * p + 1) for p in range(HEADS // 2)]
    egl_row = egl_ref[0:1, :]
    ph = [jnp.dot(lw_ref[h], s_ref[:, GDN_DV * h:GDN_DV * (h + 1)].astype(BF16), preferred_element_type=F32)
          for h in range(HEADS)]
    yield
    vnew = [u_ref[CHUNK * h:CHUNK * (h + 1), :] - ph[h][CHUNK:] for h in range(HEADS)]
    intra, upd = [], []
    for p, heads in enumerate(pairs):
        v0, v1 = vnew[heads[0]], vnew[heads[1]]
        intra.append(_mm(attn_ref[p], jnp.concatenate([v0, v1], axis=0)))
        zv = jnp.zeros_like(v0)
        vbd = jnp.concatenate([jnp.concatenate([v0, zv], axis=1), jnp.concatenate([zv, v1], axis=1)], axis=0)
        upd.append(_mm_tn(kdec_ref[p], vbd))
    mark_consumed()
    yield
    out = []
    for p, heads in enumerate(pairs):
        egl = jnp.concatenate(
            [jnp.broadcast_to(egl_row[:, _L_DECAY + h:_L_DECAY + h + 1], (GDN_DK, GDN_DV)) for h in heads], axis=1)
        cols = slice(2 * GDN_DV * p, 2 * GDN_DV * (p + 1))
        s_ref[:, cols] = s_ref[:, cols] * egl + upd[p]
        for idx, h in enumerate(heads):
            out.append(ph[h][:CHUNK] + intra[p][CHUNK * idx:CHUNK * (idx + 1)])
    oss = [jnp.mean(o * o, axis=-1, keepdims=True) for o in out]
    yield
    out = [out[h] * lax.rsqrt(oss[h] + NORM_EPS) * gnorm * _silu(_head_cols(z, h)) for h in range(HEADS)]
    o_ref[...] = jnp.concatenate(out, axis=1).astype(BF16)


def _mlstm_prepare(q, k, gates, cs, consts, qk_ref, dl_ref, gate_ref, consumed):
    neginclp, _, _, eyep, ones3, _ = consts
    head = _lane_head((CHUNK, PACKED))
    kb = k.astype(BF16)
    kexp = jnp.concatenate([jnp.where(head == h, kb, jnp.zeros_like(kb)) for h in range(HEADS)], axis=0)
    qk = _mm_nt(q * (ML_DQK ** -0.5), kexp)
    bcol = _expand(cs, _L_F)
    crow = _row_form(_expand(gates, _L_I) - bcol, eyep, ones3)
    yield

    dl = bcol + crow + neginclp
    lane = _lane_iota((CHUNK, LANES))
    low = lane < CHUNK
    rowmax = jnp.zeros((CHUNK, LANES), F32)
    for h in range(HEADS):
        blk = _head_cols(dl, h // 2)
        mx = jnp.max(jnp.where(low if h % 2 == 0 else ~low, blk, -jnp.inf), axis=-1, keepdims=True)
        rowmax = jnp.where(lane == _L_F + h, mx, rowmax)
    while not consumed():
        yield
    qk_ref[...] = qk
    dl_ref[...] = dl
    gate_ref[0] = cs
    gate_ref[1] = cs[CHUNK - 1:CHUNK, :] - cs + pltpu.roll(gates, _L_F - _L_I, 1)
    gate_ref[2] = rowmax


def _mlstm_scan(q, k, v, og, mnorm, c_ref, m_ref, qk_ref, dl_ref, gate_ref, o_ref, mark_consumed):
    cs, a, rowmax = gate_ref[0], gate_ref[1], gate_ref[2]
    m_row = m_ref[0:1, :]
    m_inter = cs + m_row
    m_t = jnp.maximum(m_inter, rowmax)
    w_inter = jnp.exp(m_inter - m_t)
    emt = jnp.exp(-m_t)
    s = qk_ref[...] * jnp.exp(dl_ref[...] - _expand(m_t, _L_F))
    mark_consumed()
    qw = q * (ML_DQK ** -0.5) * _expand(w_inter, _L_F)

    b_last = cs[CHUNK - 1:CHUNK, :]
    m_new = jnp.maximum(b_last + m_row, jnp.max(a, axis=0, keepdims=True))
    scale_row = jnp.exp(b_last + m_row - m_new)
    kw = k * _expand(jnp.exp(a - m_new), _L_F)
    m_ref[...] = jnp.broadcast_to(m_new, m_ref.shape)

    ones_col = (_lane_iota((CHUNK, ML_DV)) == 0).astype(F32)
    pairs = [(2 * p, 2 * p + 1) for p in range(HEADS // 2)]
    acc, upd = [], []
    for p, heads in enumerate(pairs):
        vext = jnp.concatenate(
            [jnp.concatenate([_head_cols(v, h), ones_col], axis=1) for h in heads], axis=0)
        lhs = jnp.concatenate([_pair_rows(qw, p), _pair_rows(s, p)], axis=1)
        acc.append(_mm(lhs, jnp.concatenate([c_ref[p], vext], axis=0)))
        upd.append(_mm_tn(_pair_rows(kw, p), vext))
    yield
    out = []
    for p, heads in enumerate(pairs):
        scale = jnp.concatenate(
            [jnp.broadcast_to(scale_row[:, _L_F + h:_L_F + h + 1], (ML_DQK, 2 * ML_DV)) for h in heads], axis=0)
        c_ref[p] = c_ref[p] * scale + upd[p]
    den = []
    for h in range(HEADS):
        rows = slice(CHUNK * (h % 2), CHUNK * (h % 2 + 1))
        d = jnp.maximum(jnp.abs(acc[h // 2][rows, ML_DV:ML_DV + 1]), emt[:, _L_F + h:_L_F + h + 1])
        den.append(jnp.broadcast_to(d, (CHUNK, ML_DV)))
    yield
    out = [acc[h // 2][CHUNK * (h % 2):CHUNK * (h % 2 + 1), :ML_DV] / den[h] for h in range(HEADS)]
    oss = [jnp.mean(o * o, axis=-1, keepdims=True) for o in out]
    yield
    out = [out[h] * lax.rsqrt(oss[h] + NORM_EPS) * _head_cols(mnorm, h) * _sigmoid(_head_cols(og, h))
           for h in range(HEADS)]
    o_ref[...] = jnp.concatenate(out, axis=1).astype(BF16)


def _mixer_kernel(gq_ref, gk_ref, gv_ref, mqka_ref, sm_ref, gqp_ref, gkp_ref, gvp_ref, gz_ref, mqkb_ref, mv_ref, mo_ref,
                  convw_ref, bias_ref, alog_ref, gnorm_ref, mnorm_ref,
                  negincl_ref, strict_ref, d16_ref, eye_ref, l3_ref, headsum_ref, shift_ref,
                  oa_ref, ob_ref, s_ref, c_ref, m_ref,
                  lw_ref, u_ref, attn_ref, kdec_ref, egl_ref, qk_ref, dl_ref, gate_ref):
    nb, n_rows = gq_ref.shape[0], gq_ref.shape[1]
    subs = n_rows // CHUNK

    @pl.when(pl.program_id(0) == 0)
    def _():
        for ref in (s_ref, c_ref, m_ref, lw_ref, u_ref, attn_ref, kdec_ref, egl_ref, qk_ref, dl_ref):
            ref[...] = jnp.zeros_like(ref)
        for bi in range(nb):
            for s in range(subs):
                gate_ref[bi, s, 0] = jnp.zeros((CHUNK, SMALL_W), F32)
                gate_ref[bi, s, 1] = jnp.full((CHUNK, SMALL_W), -jnp.inf, F32)
                gate_ref[bi, s, 2] = jnp.zeros((CHUNK, SMALL_W), F32)

    cw = convw_ref[...]
    gnorm = gnorm_ref[...]
    mnorm = mnorm_ref[...]
    ones3 = jnp.ones((CHUNK, 3 * CHUNK), BF16)
    consts = (negincl_ref[...], strict_ref[...], d16_ref[...], eye_ref[...], ones3, headsum_ref[...])
    lane = _lane_iota((n_rows, SMALL_W))
    decay_lanes = (lane >= _L_DECAY) & (lane < _L_DECAY + HEADS)
    f_lanes = (lane >= _L_F) & (lane < _L_F + HEADS)
    neg_a = -jnp.exp(alog_ref[...])
    hq = HEADS * GDN_DK
    hm = HEADS * ML_DQK
    chunk_rows = [slice(CHUNK * s, CHUNK * (s + 1)) for s in range(subs)]
    consumed = set()

    def scans(bi):
        z, mqk = gz_ref[bi].astype(F32), mqkb_ref[bi].astype(F32)
        v, og = mv_ref[bi].astype(F32), mo_ref[bi].astype(F32)
        for s, r in enumerate(chunk_rows):
            yield from _round_robin([
                _gdn_scan(z[r], gnorm, s_ref.at[bi], lw_ref.at[bi, s], u_ref.at[bi, s], attn_ref.at[bi, s],
                          kdec_ref.at[bi, s], egl_ref.at[bi, s], oa_ref.at[bi, r],
                          functools.partial(consumed.add, ("gdn", bi, s))),
                _mlstm_scan(mqk[r, :hm], mqk[r, hm:], v[r], og[r], mnorm, c_ref.at[bi], m_ref.at[bi],
                            qk_ref.at[bi, s], dl_ref.at[bi, s], gate_ref.at[bi, s], ob_ref.at[bi, r],
                            functools.partial(consumed.add, ("mlstm", bi, s)))])
            yield

    def prepare(bi):
        xb = jnp.concatenate([gq_ref[bi], gk_ref[bi], gv_ref[bi]], axis=1)
        xp = jnp.concatenate([gqp_ref[bi], gkp_ref[bi], gvp_ref[bi]], axis=1)
        xp = jnp.where(pl.program_id(0) > 0, xp, jnp.zeros_like(xp))
        xe = jnp.concatenate([xp, xb], axis=0)
        shifted = jnp.dot(shift_ref[...], xe, preferred_element_type=F32)
        y = cw[CONV_WIDTH - 1:CONV_WIDTH, :] * shifted[:n_rows]
        for tap in range(1, CONV_WIDTH):
            y = y + cw[CONV_WIDTH - 1 - tap:CONV_WIDTH - tap, :] * shifted[n_rows * tap:n_rows * (tap + 1)]
        qkv = _silu(y)

        t = sm_ref[bi] + bias_ref[...]
        sp = _softplus(t)
        beta_blk = _sigmoid(t)
        cs_in = jnp.where(f_lanes, t - sp, jnp.where(decay_lanes, neg_a * sp, 0.0))
        cs = [jnp.dot(l3_ref[...], _split3_rows(cs_in[r]), preferred_element_type=F32)
              for r in chunk_rows]
        yield
        mqk = mqka_ref[bi].astype(F32)
        streams = []
        for s, r in enumerate(chunk_rows):
            streams.append(_gdn_prepare(
                qkv[r, :hq], qkv[r, hq:2 * hq], qkv[r, 2 * hq:], beta_blk[r], cs[s], consts,
                lw_ref.at[bi, s], u_ref.at[bi, s], attn_ref.at[bi, s], kdec_ref.at[bi, s], egl_ref.at[bi, s],
                functools.partial(consumed.__contains__, ("gdn", bi, s))))
            streams.append(_mlstm_prepare(
                mqk[r, :hm], mqk[r, hm:], t[r], cs[s], consts, qk_ref.at[bi, s], dl_ref.at[bi, s],
                gate_ref.at[bi, s], functools.partial(consumed.__contains__, ("mlstm", bi, s))))
        yield from _round_robin(streams)

    for _ in _round_robin([prepare(bi) for bi in range(nb)] + [scans(bi) for bi in range(nb)]):
        pass


def _mixer_constants():
    i = np.arange(CHUNK)[:, None]
    j = np.tile(np.arange(CHUNK), HEADS)[None, :]
    negincl = np.where(i >= j, 0.0, -np.inf).astype(np.float32)
    strict = (i > j).astype(np.float32)
    d16 = ((i // SUB) == (j // SUB)).astype(np.float32)
    eye = (i == j).astype(np.float32)
    lower = (np.arange(CHUNK)[:, None] >= np.arange(CHUNK)[None, :]).astype(np.float32)
    l3 = np.tile(lower, (1, 3))
    headsum = (np.arange(2 * HEADS * GDN_DK)[:, None] // GDN_DK == np.arange(SMALL_W)[None, :]).astype(np.float32)
    return (jnp.asarray(negincl), jnp.asarray(strict), jnp.asarray(d16), jnp.asarray(eye),
            jnp.asarray(l3, dtype=BF16), jnp.asarray(headsum, dtype=BF16))


def _mixer(proj, small, conv_w, bias_row, alog_row, gnorm, mnorm, consts):
    _, bsz, seq, _ = proj.shape
    subs = MIXER_SUBS if seq % (MIXER_SUBS * CHUNK) == 0 else 1
    rows = subs * CHUNK
    n_blocks = seq // rows
    conv_ch = conv_w.shape[1]

    def prep(c):
        return jnp.minimum(c, n_blocks - 1)

    def scan(c):
        return jnp.maximum(c - 1, 0)

    def pblk(col, block_of):
        return pl.BlockSpec((None, bsz, rows, COL), lambda c: (col, 0, block_of(c), 0))

    def const(a):
        return pl.BlockSpec(a.shape, lambda c: (0,) * a.ndim)

    pair = HEADS // 2
    tap, row = np.divmod(np.arange(CONV_WIDTH * rows), rows)
    shift = (np.arange(2 * rows)[None, :] == (rows + row - tap)[:, None]).astype(np.float32)
    fixed = (conv_w, bias_row, alog_row, gnorm, mnorm) + tuple(consts) + (jnp.asarray(shift, dtype=BF16),)
    return pl.pallas_call(
        _mixer_kernel,
        grid=(n_blocks + 1,),
        in_specs=[pblk(_GQ, prep), pblk(_GK, prep), pblk(_GV, prep), pblk(_MQK, prep),
                  pl.BlockSpec((bsz, rows, SMALL_W), lambda c: (0, prep(c), 0)),
                  pblk(_GQ, scan), pblk(_GK, scan), pblk(_GV, scan),
                  pblk(_GZ, scan), pblk(_MQK, scan), pblk(_MV, scan), pblk(_MO, scan)]
                 + [const(a) for a in fixed],
        out_specs=[pl.BlockSpec((bsz, rows, HEADS * GDN_DV), lambda c: (0, scan(c), 0)),
                   pl.BlockSpec((bsz, rows, HEADS * ML_DV), lambda c: (0, scan(c), 0))],
        out_shape=[jax.ShapeDtypeStruct((bsz, seq, HEADS * GDN_DV), BF16),
                   jax.ShapeDtypeStruct((bsz, seq, HEADS * ML_DV), BF16)],
        scratch_shapes=[
            pltpu.VMEM((bsz, GDN_DK, HEADS * GDN_DV), F32),
            pltpu.VMEM((bsz, pair, 2 * ML_DQK, 2 * ML_DV), F32),
            pltpu.VMEM((bsz, 8, SMALL_W), F32),
            pltpu.VMEM((bsz, subs, HEADS, 2 * CHUNK, GDN_DK), BF16),
            pltpu.VMEM((bsz, subs, HEADS * CHUNK, GDN_DV), F32),
            pltpu.VMEM((bsz, subs, pair, 2 * CHUNK, LANES), BF16),
            pltpu.VMEM((bsz, subs, pair, 2 * CHUNK, GDN_DK), BF16),
            pltpu.VMEM((bsz, subs, 8, SMALL_W), F32),
            pltpu.VMEM((bsz, subs, CHUNK, PACKED), F32),
            pltpu.VMEM((bsz, subs, CHUNK, PACKED), F32),
            pltpu.VMEM((bsz, subs, 3, CHUNK, SMALL_W), F32),
        ],
        compiler_params=pltpu.CompilerParams(
            dimension_semantics=("arbitrary",), vmem_limit_bytes=VMEM_LIMIT),
        name="mixer",
    )(proj, proj, proj, proj, small, proj, proj, proj, proj, proj, proj, proj, *fixed)


def _tail_kernel(oa_ref, ob_ref, ga0_ref, ga1_ref, gb0_ref, gb1_ref, x_ref, p_ref,
                 wa_ref, wb_ref, wo_ref, w1_ref, w3_ref, w2_ref, wpg_ref, wple_ref,
                 gffn_ref, gple_ref, gfin_ref, out_ref, *, final, splits):
    ya = jnp.dot(oa_ref[...], wa_ref[...], preferred_element_type=F32)
    yb = jnp.dot(ob_ref[...], wb_ref[...], preferred_element_type=F32)
    ga = jnp.concatenate([ga0_ref[...], ga1_ref[...]], axis=1).astype(F32)
    gb = jnp.concatenate([gb0_ref[...], gb1_ref[...]], axis=1).astype(F32)
    mixed = _sigmoid(ga) * ya + _sigmoid(gb) * yb
    x1 = x_ref[...] + jnp.dot(mixed.astype(BF16), wo_ref[...], preferred_element_type=F32)

    h = _rms(x1, gffn_ref[...]).astype(BF16)
    x2 = x1
    for lo, hi in splits:
        a = jnp.dot(h, w1_ref[:, lo:hi], preferred_element_type=F32)
        b = jnp.dot(h, w3_ref[:, lo:hi], preferred_element_type=F32)
        x2 = x2 + jnp.dot((_silu(a) * b).astype(BF16), w2_ref[lo:hi, :], preferred_element_type=F32)

    gate = _sigmoid(jnp.dot(_rms(x2, gple_ref[...]).astype(BF16), wpg_ref[...], preferred_element_type=F32))
    x3 = x2 + gate * jnp.dot(p_ref[...].astype(BF16), wple_ref[...], preferred_element_type=F32)
    if final:
        x3 = _rms(x3, gfin_ref[...])
    out_ref[...] = x3


def _tail(oa, ob, proj, x, p, weights, gains, layer, tm, final):
    t, d = x.shape
    dff = weights[3].shape[2]
    half = (dff // MXU_WIDTH // 2) * MXU_WIDTH
    splits = ((0, half), (half, dff))
    steps = t // tm

    def rows(width):
        return pl.BlockSpec((tm, width), lambda i: (i, 0))

    def gate(col):
        return pl.BlockSpec((None, tm, COL), lambda i: (col, i, 0))

    def resident(a):
        return pl.BlockSpec(a.shape, lambda i: (0, 0), pipeline_mode=pl.Buffered(1))

    return pl.pallas_call(
        functools.partial(_tail_kernel, final=final, splits=splits),
        grid=(steps,),
        in_specs=[rows(oa.shape[1]), rows(ob.shape[1]), gate(_GA), gate(_GA + 1), gate(_GB), gate(_GB + 1),
                  rows(d), pl.BlockSpec((tm, p.shape[1]), lambda i: (layer * steps + i, 0))]
                 + [_layer_block(a, layer) for a in weights] + [resident(a) for a in gains],
        out_specs=rows(d),
        out_shape=jax.ShapeDtypeStruct((t, d), F32),
        compiler_params=pltpu.CompilerParams(
            dimension_semantics=("parallel",), vmem_limit_bytes=VMEM_LIMIT),
        name="tail_final" if final else "tail",
    )(oa, ob, proj, proj, proj, proj, x, p, *weights, *gains)


def _pack_kernel(wt_ref, o_ref):
    o_ref[...] = wt_ref[0].T.astype(BF16)


def _pack_small_kernel(a_ref, b_ref, o_ref, *, a_lanes, b_lanes):
    a, b = a_ref[0].T, b_ref[0].T
    lane = _lane_iota(a.shape)
    o_ref[...] = jnp.where(lane < a_lanes, a, jnp.where(lane < b_lanes, b, 0.0)).astype(BF16)


def _pack_w_in(w_in):
    depth, d, width = w_in.shape
    g_w, m_w = 4 * HEADS * GDN_DK, 2 * HEADS * ML_DQK + 2 * HEADS * ML_DV
    m0 = g_w + 2 * HEADS
    gates0 = m0 + m_w + 2 * HEADS
    assert gates0 + 2 * d == width and g_w == (_GZ + 1 - _GQ) * COL and 2 * d == _GQ * COL
    assert gates0 % 8 == 0 and m0 % 8 == 0 and 2 * HEADS == 8
    wt = jnp.swapaxes(w_in, 1, 2)

    def src_row(k):
        row = jnp.where(k < _GQ, gates0 + COL * k, jnp.where(k < _MQK, COL * (k - _GQ), m0 + COL * (k - _MQK)))
        return pl.multiple_of(row, 2 * HEADS)

    params = pltpu.CompilerParams(dimension_semantics=("parallel", "parallel"), vmem_limit_bytes=VMEM_LIMIT)
    main = pl.pallas_call(
        _pack_kernel,
        grid=(depth, PACK_W // COL),
        in_specs=[pl.BlockSpec((pl.Element(1), pl.Element(COL), pl.Element(d)), lambda l, k: (l, src_row(k), 0))],
        out_specs=pl.BlockSpec((None, d, COL), lambda l, k: (l, 0, k)),
        out_shape=jax.ShapeDtypeStruct((depth, d, PACK_W), BF16),
        compiler_params=params,
        name="pack_w_in",
    )(wt)
    i0 = m0 + m_w
    assert g_w % LANES == 0 and i0 % LANES == _L_I

    def window(row0):
        return pl.BlockSpec((pl.Element(1), pl.Element(SMALL_W), pl.Element(d)), lambda l, _: (l, row0, 0))

    small = pl.pallas_call(
        functools.partial(_pack_small_kernel, a_lanes=_L_I, b_lanes=_L_I + 2 * HEADS),
        grid=(depth, 1),
        in_specs=[window(g_w), window(i0 - _L_I)],
        out_specs=pl.BlockSpec((None, d, SMALL_W), lambda l, _: (l, 0, 0)),
        out_shape=jax.ShapeDtypeStruct((depth, d, SMALL_W), BF16),
        compiler_params=params,
        name="pack_w_small",
    )(wt, wt)
    return main, small


def _gate_rows(values, lane0):
    return jnp.pad(values, ((0, 0), (lane0, SMALL_W - lane0 - values.shape[1])))[:, None, :]


def kernel(x, p, g_mix, w_in, conv_w, a_log, dt_bias, gdn_norm, ml_i_bias, ml_f_bias, ml_norm, w_branch_a,
           w_branch_b, w_out, g_ffn, w1, w3, w2, g_ple, w_ple_gate, w_ple, g_final):
    bsz, seq, d = x.shape
    depth = w_in.shape[0]
    t = bsz * seq
    assert w_in.shape[2] == PACK_W + 4 * HEADS and seq % CHUNK == 0

    w_main, w_small = _pack_w_in(w_in)
    wa, wb, wo = w_branch_a.astype(BF16), w_branch_b.astype(BF16), w_out.astype(BF16)
    w1b, w3b, w2b = w1.astype(BF16), w3.astype(BF16), w2.astype(BF16)
    wpg, wple = w_ple_gate.astype(BF16), w_ple.astype(BF16)
    bias_rows = _gate_rows(dt_bias, _L_DECAY) + _gate_rows(ml_i_bias, _L_I) + _gate_rows(ml_f_bias, _L_F)
    alog_rows = _gate_rows(a_log, _L_DECAY)
    gnorm = gdn_norm[:, None, :]
    mnorm = ml_norm.reshape(depth, 1, HEADS * ML_DV)
    consts = _mixer_constants()

    tm_in, tm_tail = min(1024, t), min(512, t)
    xt = x.reshape(t, d)
    p_rows = p.reshape(depth * t, p.shape[-1])
    weights = (wa, wb, wo, w1b, w3b, w2b, wpg, wple)
    for i in range(depth):
        proj, small = _inproj(xt, g_mix[i][None, :], w_main, w_small, i, tm_in)
        oa, ob = _mixer(proj.reshape(-1, bsz, seq, COL), small.reshape(bsz, seq, SMALL_W), conv_w[i],
                        bias_rows[i], alog_rows[i], gnorm[i], mnorm[i], consts)
        gains = (g_ffn[i][None, :], g_ple[i][None, :], g_final[None, :])
        xt = _tail(oa.reshape(t, -1), ob.reshape(t, -1), proj, xt, p_rows, weights, gains, i,
                   tm_tail, final=(i == depth - 1))
    return xt.reshape(bsz, seq, d)
```
